```python
import math
import jax, jax.numpy as jnp
from jax import lax
import numpy as np

D_MODEL = 1024
BATCH = 8
SEQ = 4096
DEPTH = 2

HEAD_DIM = 64
N_HEAD_SLOTS = D_MODEL // HEAD_DIM
MIX_WIDTH = N_HEAD_SLOTS * HEAD_DIM
HEADS_PER_MIXER = N_HEAD_SLOTS // 2
KV_HEADS = 2
GROUP = HEADS_PER_MIXER // KV_HEADS
HQ = HEADS_PER_MIXER * HEAD_DIM
KVW = KV_HEADS * HEAD_DIM
Q_BLOCK = 128
NSA_CMP_LEN = 32
NSA_CMP_STRIDE = 16
NSA_SEL_LEN = 64
NSA_SEL_TOP = 8
NSA_WINDOW = 512
NSA_FORCE = 1e4
SWA_WINDOW = 128
MOBA_BLOCK = 256
MOBA_TOP = 3
MOBA_Q_CHUNK = 32
REL_BUCKETS = 32
REL_MAX_DIST = 128
D_FF = ((8 * D_MODEL // 3 + 255) // 256) * 256
RMS_EPS = 1e-6
NEG = -1e30
AB_WIDTH = HQ + 6 * KVW + 3 * HEADS_PER_MIXER + HQ + 2 * KVW
CD_WIDTH = HQ + 2 * KVW + 3 * HQ
N_EVEN = (DEPTH + 1) // 2
N_ODD = DEPTH // 2

kernel_name = "hybrid_nsa_swa_moba_stickbreak_block"


def rms_norm(x, w):
    xf = x.astype(jnp.float32)
    y = xf * lax.rsqrt(jnp.mean(xf * xf, axis=-1, keepdims=True) + RMS_EPS)
    return (y * w.astype(jnp.float32)).astype(x.dtype)


def rel_bucket(dist):
    n = jnp.maximum(dist, 0)
    max_exact = REL_BUCKETS // 2
    nf = jnp.maximum(n, 1).astype(jnp.float32)
    large = max_exact + (jnp.log(nf / max_exact) / math.log(REL_MAX_DIST / max_exact)
                         * (REL_BUCKETS - max_exact)).astype(jnp.int32)
    large = jnp.minimum(large, REL_BUCKETS - 1)
    return jnp.where(n < max_exact, n, large)


def masked_softmax(s, valid):
    s = jnp.where(valid, s.astype(jnp.float32), NEG)
    p = jax.nn.softmax(s, axis=-1)
    return jnp.where(valid, p, 0.0)


def nsa_attention(q, k_cmp, v_cmp, k_sel, v_sel, k_win, v_win, gates, cmp_wk, cmp_wv, cmp_pe, rel_tab):
    B, S, H, dh = q.shape
    G, R = KV_HEADS, GROUP
    dt = q.dtype
    scale = HEAD_DIM ** -0.5
    n_cmp = (S - NSA_CMP_LEN) // NSA_CMP_STRIDE + 1
    cmp_start = jnp.arange(n_cmp) * NSA_CMP_STRIDE
    gidx = cmp_start[:, None] + jnp.arange(NSA_CMP_LEN)[None, :]
    kc = jnp.einsum('bnlgd,lde->bnge', k_cmp[:, gidx] + cmp_pe[:, None, :], cmp_wk)
    vc = jnp.einsum('bnlgd,lde->bnge', v_cmp[:, gidx] + cmp_pe[:, None, :], cmp_wv)
    cmp_end = cmp_start + NSA_CMP_LEN - 1
    n_sel = S // NSA_SEL_LEN
    sel_start = jnp.arange(n_sel) * NSA_SEL_LEN
    overlap = jnp.maximum(jnp.minimum(cmp_end[:, None] + 1, sel_start[None, :] + NSA_SEL_LEN)
                          - jnp.maximum(cmp_start[:, None], sel_start[None, :]), 0)
    overlap_w = overlap.astype(jnp.float32) / NSA_CMP_LEN
    top = min(NSA_SEL_TOP, n_sel)
    ksb = k_sel.reshape(B, n_sel, NSA_SEL_LEN, G, dh).transpose(0, 3, 1, 2, 4)
    vsb = v_sel.reshape(B, n_sel, NSA_SEL_LEN, G, dh).transpose(0, 3, 1, 2, 4)
    kw = jnp.pad(k_win, ((0, 0), (NSA_WINDOW, 0), (0, 0), (0, 0)))
    vw = jnp.pad(v_win, ((0, 0), (NSA_WINDOW, 0), (0, 0), (0, 0)))
    bi = jnp.arange(B)[:, None, None, None]
    gi = jnp.arange(G)[None, :, None, None]
    sb = jnp.arange(n_sel)

    def block(n):
        t0 = n * Q_BLOCK
        tq = t0 + jnp.arange(Q_BLOCK)
        qb = lax.dynamic_slice_in_dim(q, t0, Q_BLOCK, axis=1).reshape(B, Q_BLOCK, G, R, dh)
        gb = lax.dynamic_slice_in_dim(gates, t0, Q_BLOCK, axis=1).reshape(B, Q_BLOCK, G, R, 1, 3)
        dist = tq[:, None] - cmp_end[None, :]
        s = jnp.einsum('bqgrd,bngd->bgrqn', qb, kc) * scale
        s = s + rel_tab[rel_bucket(dist)].transpose(2, 3, 0, 1)
        p_cmp = masked_softmax(s, dist >= 0)
        o_cmp = jnp.einsum('bgrqn,bngd->bqgrd', p_cmp.astype(dt), vc)
        imp = jnp.einsum('bgrqn,ns->bgqs', p_cmp, overlap_w)
        own = tq // NSA_SEL_LEN
        forced = (sb[None, :] == 0) | (sb[None, :] == own[:, None]) | (sb[None, :] == own[:, None] - 1)
        causal_blk = sb[None, :] <= own[:, None]
        imp = jnp.where(causal_blk, imp + jnp.where(forced, NSA_FORCE, 0.0), NEG)
        _, sel = lax.top_k(imp, top)
        ks = ksb[bi, gi, sel].reshape(B, G, Q_BLOCK, top * NSA_SEL_LEN, dh)
        vs = vsb[bi, gi, sel].reshape(B, G, Q_BLOCK, top * NSA_SEL_LEN, dh)
        pos = (sel[..., None] * NSA_SEL_LEN + jnp.arange(NSA_SEL_LEN)).reshape(B, G, Q_BLOCK, top * NSA_SEL_LEN)
        dist = tq[None, None, :, None] - pos
        s = jnp.einsum('bqgrd,bgqkd->bgrqk', qb, ks) * scale
        s = s + rel_tab[rel_bucket(dist), gi].transpose(0, 1, 4, 2, 3)
        p = masked_softmax(s, (dist >= 0)[:, :, None])
        o_sel = jnp.einsum('bgrqk,bgqkd->bqgrd', p.astype(dt), vs)
        kwb = lax.dynamic_slice_in_dim(kw, t0, NSA_WINDOW + Q_BLOCK, axis=1)
        vwb = lax.dynamic_slice_in_dim(vw, t0, NSA_WINDOW + Q_BLOCK, axis=1)
        kpos = t0 - NSA_WINDOW + jnp.arange(NSA_WINDOW + Q_BLOCK)
        dist = tq[:, None] - kpos[None, :]
        valid = (dist >= 0) & (dist < NSA_WINDOW) & (kpos[None, :] >= 0)
        s = jnp.einsum('bqgrd,bkgd->bgrqk', qb, kwb) * scale
        s = s + rel_tab[rel_bucket(dist)].transpose(2, 3, 0, 1)
        p = masked_softmax(s, valid)
        o_win = jnp.einsum('bgrqk,bkgd->bqgrd', p.astype(dt), vwb)
        o = jnp.stack([o_cmp, o_sel, o_win], axis=-1)
        return jnp.sum(o * gb, axis=-1).reshape(B, Q_BLOCK, H * dh)

    out = lax.map(block, jnp.arange(S // Q_BLOCK))
    return out.transpose(1, 0, 2, 3).reshape(B, S, H * dh)


def swa_sink_attention(q, k, v, sinks, rel_tab):
    B, S, H, dh = q.shape
    G, R = KV_HEADS, GROUP
    dt = q.dtype
    scale = HEAD_DIM ** -0.5
    kp = jnp.pad(k, ((0, 0), (SWA_WINDOW, 0), (0, 0), (0, 0)))
    vp = jnp.pad(v, ((0, 0), (SWA_WINDOW, 0), (0, 0), (0, 0)))
    sink = jnp.broadcast_to(sinks.astype(jnp.float32).reshape(G, R)[None, :, :, None, None],
                            (B, G, R, Q_BLOCK, 1))

    def block(n):
        t0 = n * Q_BLOCK
        tq = t0 + jnp.arange(Q_BLOCK)
        qb = lax.dynamic_slice_in_dim(q, t0, Q_BLOCK, axis=1).reshape(B, Q_BLOCK, G, R, dh)
        kb = lax.dynamic_slice_in_dim(kp, t0, SWA_WINDOW + Q_BLOCK, axis=1)
        vb = lax.dynamic_slice_in_dim(vp, t0, SWA_WINDOW + Q_BLOCK, axis=1)
        kpos = t0 - SWA_WINDOW + jnp.arange(SWA_WINDOW + Q_BLOCK)
        dist = tq[:, None] - kpos[None, :]
        valid = (dist >= 0) & (dist < SWA_WINDOW) & (kpos[None, :] >= 0)
        s = jnp.einsum('bqgrd,bkgd->bgrqk', qb, kb) * scale
        s = s + rel_tab[rel_bucket(dist)].transpose(2, 3, 0, 1)
        s = jnp.where(valid, s.astype(jnp.float32), NEG)
        p = jax.nn.softmax(jnp.concatenate([s, sink], axis=-1), axis=-1)[..., :-1]
        o = jnp.einsum('bgrqk,bkgd->bqgrd', p.astype(dt), vb)
        return o.reshape(B, Q_BLOCK, H * dh)

    out = lax.map(block, jnp.arange(S // Q_BLOCK))
    return out.transpose(1, 0, 2, 3).reshape(B, S, H * dh)


def moba_attention(q, k, v, rel_tab_h):
    B, S, H, dh = q.shape
    G, R = KV_HEADS, GROUP
    L = MOBA_BLOCK
    dt = q.dtype
    scale = HEAD_DIM ** -0.5
    nb = max(-(-S // L), 2)
    pad = nb * L - S
    kp = jnp.pad(k, ((0, 0), (0, pad), (0, 0), (0, 0)))
    vp = jnp.pad(v, ((0, 0), (0, pad), (0, 0), (0, 0)))
    head_grp = jnp.arange(H) // R
    kb = kp.reshape(B, nb, L, G, dh)
    k_mean = jnp.mean(kb.astype(jnp.float32), axis=2).astype(dt)[:, :, head_grp]
    kbh = kb[:, :, :, head_grp].transpose(0, 3, 1, 2, 4)
    vbh = vp.reshape(B, nb, L, G, dh)[:, :, :, head_grp].transpose(0, 3, 1, 2, 4)
    top = min(MOBA_TOP, nb - 1)
    bi = jnp.arange(B)[:, None, None, None]
    hi = jnp.arange(H)[None, None, :, None]
    Qc = MOBA_Q_CHUNK

    def chunk(n):
        t0 = n * Qc
        tq = t0 + jnp.arange(Qc)
        qc = lax.dynamic_slice_in_dim(q, t0, Qc, axis=1)
        own = t0 // L
        gs = jnp.einsum('bqhd,bnhd->bqhn', qc, k_mean).astype(jnp.float32)
        gs = jnp.where(jnp.arange(nb) < own, gs, NEG)
        _, sel = lax.top_k(gs, top)
        sel_ok = sel < own
        ks = kbh[bi, hi, sel]
        vs = vbh[bi, hi, sel]
        pos = sel[..., None] * L + jnp.arange(L)
        s_past = jnp.einsum('bqhd,bqhnld->bqhnl', qc, ks) * scale
        s_past = s_past + rel_tab_h[rel_bucket(tq[None, :, None, None, None] - pos), hi[..., None]]
        k_own = lax.dynamic_slice_in_dim(kp, own * L, L, axis=1)[:, :, head_grp]
        v_own = lax.dynamic_slice_in_dim(vp, own * L, L, axis=1)[:, :, head_grp]
        dist_own = tq[:, None] - (own * L + jnp.arange(L))[None, :]
        s_own = jnp.einsum('bqhd,blhd->bqhl', qc, k_own) * scale
        s_own = s_own + rel_tab_h[rel_bucket(dist_own)].transpose(0, 2, 1)
        s = jnp.concatenate([s_past.reshape(B, Qc, H, top * L), s_own], axis=-1)
        valid = jnp.concatenate([
            jnp.broadcast_to(sel_ok[..., None], (B, Qc, H, top, L)).reshape(B, Qc, H, top * L),
            jnp.broadcast_to((dist_own >= 0)[:, None, :], (B, Qc, H, L))], axis=-1)
        p = masked_softmax(s, valid).astype(dt)
        o = (jnp.einsum('bqhk,bqhkd->bqhd', p[..., :top * L], vs.reshape(B, Qc, H, top * L, dh))
             + jnp.einsum('bqhl,blhd->bqhd', p[..., top * L:], v_own))
        return o.reshape(B, Qc, H * dh)

    out = lax.map(chunk, jnp.arange(S // Qc))
    return out.transpose(1, 0, 2, 3).reshape(B, S, H * dh)


def stick_breaking_attention(q, k, v):
    B, S, H, dh = q.shape
    dt = q.dtype
    scale = HEAD_DIM ** -0.5
    kpos = jnp.arange(S)

    def block(n):
        t0 = n * Q_BLOCK
        tq = t0 + jnp.arange(Q_BLOCK)
        qb = lax.dynamic_slice_in_dim(q, t0, Q_BLOCK, axis=1)
        z = jnp.einsum('bqhd,bkhd->bhqk', qb, k).astype(jnp.float32) * scale
        strict = kpos[None, :] < tq[:, None]
        log_1m = jnp.where(strict, jax.nn.log_sigmoid(-z), 0.0)
        after = lax.cumsum(log_1m, axis=3, reverse=True) - log_1m
        w = jnp.where(strict, jnp.exp(jax.nn.log_sigmoid(z) + after), 0.0)
        o = jnp.einsum('bhqk,bkhd->bqhd', w.astype(dt), v)
        return o.reshape(B, Q_BLOCK, H * dh)

    out = lax.map(block, jnp.arange(S // Q_BLOCK))
    return out.transpose(1, 0, 2, 3).reshape(B, S, H * dh)


def mixer_ab(h, w_in, w_out, cmp_wk, cmp_wv, cmp_pe, sinks, rel_table):
    B, S, _ = h.shape
    H, G, dh = HEADS_PER_MIXER, KV_HEADS, HEAD_DIM
    proj = h @ w_in
    sizes = [HQ, KVW, KVW, KVW, KVW, KVW, KVW, 3 * H, HQ, KVW, KVW]
    qa, kca, vca, ksa, vsa, kwa, vwa, ga, qb, kb, vb = jnp.split(proj, np.cumsum(sizes)[:-1].tolist(), axis=-1)
    kv = lambda t: t.reshape(B, S, G, dh)
    gates = jax.nn.sigmoid(ga.astype(jnp.float32)).astype(h.dtype).reshape(B, S, H, 3)
    tab_a = rel_table[:, :H].reshape(REL_BUCKETS, G, GROUP)
    tab_b = rel_table[:, H:2 * H].reshape(REL_BUCKETS, G, GROUP)
    o_a = nsa_attention(qa.reshape(B, S, H, dh), kv(kca), kv(vca), kv(ksa), kv(vsa), kv(kwa), kv(vwa),
                        gates, cmp_wk, cmp_wv, cmp_pe, tab_a)
    o_b = swa_sink_attention(qb.reshape(B, S, H, dh), kv(kb), kv(vb), sinks, tab_b)
    return jnp.concatenate([o_a, o_b], axis=-1) @ w_out


def mixer_cd(h, w_in, w_out, rel_table):
    B, S, _ = h.shape
    H, G, dh = HEADS_PER_MIXER, KV_HEADS, HEAD_DIM
    proj = h @ w_in
    sizes = [HQ, KVW, KVW, HQ, HQ, HQ]
    qc, kc, vc, qd, kd, vd = jnp.split(proj, np.cumsum(sizes)[:-1].tolist(), axis=-1)
    o_c = moba_attention(qc.reshape(B, S, H, dh), kc.reshape(B, S, G, dh), vc.reshape(B, S, G, dh),
                         rel_table[:, :H])
    o_d = stick_breaking_attention(qd.reshape(B, S, H, dh), kd.reshape(B, S, H, dh), vd.reshape(B, S, H, dh))
    return jnp.concatenate([o_c, o_d], axis=-1) @ w_out


def swiglu(h, w_in, w_out):
    gate, up = jnp.split(h @ w_in, 2, axis=-1)
    return (jax.nn.silu(gate) * up) @ w_out


def setup_inputs(seed: int = 0) -> dict:
    key = jax.random.key(seed)
    ks = jax.random.split(key, 16)
    f32 = jnp.float32

    def nrm(k, shape, s):
        return jax.random.normal(k, shape, f32) * s

    return {
        "x": nrm(ks[0], (BATCH, SEQ, D_MODEL), 1.0),
        "c": nrm(ks[1], (BATCH, D_MODEL), 1.0),
        "rel_table": nrm(ks[2], (REL_BUCKETS, N_HEAD_SLOTS), 0.5),
        "mod_w": nrm(ks[3], (DEPTH, 2, D_MODEL, 3 * D_MODEL), 0.5 * D_MODEL ** -0.5),
        "mod_b": nrm(ks[4], (DEPTH, 2, 3 * D_MODEL), 0.02),
        "norm_w": 1.0 + nrm(ks[5], (DEPTH, 2, 2, D_MODEL), 0.02),
        "w_in_ab": nrm(ks[6], (N_EVEN, D_MODEL, AB_WIDTH), D_MODEL ** -0.5),
        "w_out_ab": nrm(ks[7], (N_EVEN, MIX_WIDTH, D_MODEL), MIX_WIDTH ** -0.5),
        "nsa_cmp_wk": nrm(ks[8], (N_EVEN, NSA_CMP_LEN, HEAD_DIM, HEAD_DIM), (NSA_CMP_LEN * HEAD_DIM) ** -0.5),
        "nsa_cmp_wv": nrm(ks[9], (N_EVEN, NSA_CMP_LEN, HEAD_DIM, HEAD_DIM), (NSA_CMP_LEN * HEAD_DIM) ** -0.5),
        "nsa_cmp_pe": nrm(ks[10], (N_EVEN, NSA_CMP_LEN, HEAD_DIM), 0.1),
        "swa_sinks": nrm(ks[11], (N_EVEN, HEADS_PER_MIXER), 0.5),
        "w_in_cd": nrm(ks[12], (N_ODD, D_MODEL, CD_WIDTH), D_MODEL ** -0.5),
        "w_out_cd": nrm(ks[13], (N_ODD, MIX_WIDTH, D_MODEL), MIX_WIDTH ** -0.5),
        "ffn_w_in": nrm(ks[14], (DEPTH, D_MODEL, 2 * D_FF), D_MODEL ** -0.5),
        "ffn_w_out": nrm(ks[15], (DEPTH, D_FF, D_MODEL), D_FF ** -0.5),
    }


def reference(x, c, rel_table, mod_w, mod_b, norm_w, w_in_ab, w_out_ab, nsa_cmp_wk, nsa_cmp_wv,
              nsa_cmp_pe, swa_sinks, w_in_cd, w_out_cd, ffn_w_in, ffn_w_out):
    for layer in range(DEPTH):
        shift, scale, gate = jnp.split((c @ mod_w[layer, 0] + mod_b[layer, 0])[:, None, :], 3, axis=-1)
        h = rms_norm(x, norm_w[layer, 0, 0]) * (1.0 + scale) + shift
        i = layer // 2
        if layer % 2 == 0:
            y = mixer_ab(h, w_in_ab[i], w_out_ab[i], nsa_cmp_wk[i], nsa_cmp_wv[i], nsa_cmp_pe[i],
                         swa_sinks[i], rel_table)
        else:
            y = mixer_cd(h, w_in_cd[i], w_out_cd[i], rel_table)
        x = x + gate * rms_norm(y, norm_w[layer, 0, 1])
        shift, scale, gate = jnp.split((c @ mod_w[layer, 1] + mod_b[layer, 1])[:, None, :], 3, axis=-1)
        h = rms_norm(x, norm_w[layer, 1, 0]) * (1.0 + scale) + shift
        y = swiglu(h, ffn_w_in[layer], ffn_w_out[layer])
        x = x + gate * rms_norm(y, norm_w[layer, 1, 1])
    return x
```

```python
import functools
import math

import jax
import jax.numpy as jnp
from jax import lax
from jax.experimental import pallas as pl
from jax.experimental.pallas import tpu as pltpu

F32 = jnp.float32
BF16 = jnp.bfloat16

HEAD_DIM = 64
LANES = 128
KV_HEADS = 2
GROUP = 4
HEADS_PER_MIXER = KV_HEADS * GROUP
HQ = HEADS_PER_MIXER * HEAD_DIM
KVW = KV_HEADS * HEAD_DIM
Q_BLOCK = 128
NSA_CMP_LEN = 32
NSA_CMP_STRIDE = 16
NSA_SEL_LEN = 64
NSA_SEL_TOP = 8
NSA_WINDOW = 512
NSA_FORCE = 1e4
SWA_WINDOW = 128
MOBA_BLOCK = 256
MOBA_TOP = 3
REL_BUCKETS = 32
REL_MAX_DIST = 128
RMS_EPS = 1e-6
NEG = -1e30
BELOW_NEG = -3e38
SCALE = HEAD_DIM ** -0.5
SB_TQ = 256
SB_TK = 128
VMEM_LIMIT = 56 * 1024 * 1024

_NT = (((1,), (1,)), ((), ()))


def _cparams(sem):
    return pltpu.CompilerParams(dimension_semantics=sem, vmem_limit_bytes=VMEM_LIMIT)


def _dot(a, b):
    return jnp.dot(a, b, preferred_element_type=F32)


def _dot_nt(a, b):
    return lax.dot_general(a, b, _NT, preferred_element_type=F32)


def _split2(x):
    hi = x.astype(BF16)
    lo = (x - hi.astype(F32)).astype(BF16)
    return hi, lo


def _rms(x, w):
    return x * lax.rsqrt(jnp.mean(x * x, axis=-1, keepdims=True) + RMS_EPS) * w


def _mod_kernel(c_ref, w_ref, b_ref, o_ref):
    o_ref[0] = jnp.dot(c_ref[...], w_ref[0], preferred_element_type=F32,
                       precision=lax.Precision.HIGHEST) + b_ref[0]


def _modulation(c, mod_w, mod_b):
    n, d, d3 = mod_w.shape
    b = c.shape[0]
    tn = 1024
    return pl.pallas_call(
        _mod_kernel,
        grid=(n, d3 // tn),
        in_specs=[pl.BlockSpec((b, d), lambda i, j: (0, 0)),
                  pl.BlockSpec((1, d, tn), lambda i, j: (i, 0, j)),
                  pl.BlockSpec((1, 1, tn), lambda i, j: (i, 0, j))],
        out_specs=pl.BlockSpec((1, b, tn), lambda i, j: (i, 0, j)),
        out_shape=jax.ShapeDtypeStruct((n, b, d3), F32),
        compiler_params=_cparams(("arbitrary", "arbitrary")),
        name="modulation",
    )(c, mod_w, mod_b.reshape(n, 1, d3))


def _in_proj_kernel(x_ref, nw_ref, sc_ref, sh_ref, w_ref, *out_refs, nb, nf, gate_cols):
    x = x_ref[0]
    h = _rms(x, nw_ref[...]) * (1.0 + sc_ref[0]) + sh_ref[0]
    hb = h.astype(BF16)
    ob_ref = out_refs[0]
    for c0 in range(0, nb, 512):
        c1 = min(c0 + 512, nb)
        ob_ref[0, :, c0:c1] = _dot(hb, w_ref[:, c0:c1]).astype(BF16)
    if nf:
        of_ref = out_refs[1]
        y = _dot(hb, w_ref[:, nb:nb + nf])
        if gate_cols:
            y_g = jax.nn.sigmoid(y[:, nf - gate_cols:])
            of_ref[0, :, :nf - gate_cols] = y[:, :nf - gate_cols]
            of_ref[0, :, nf - gate_cols:] = y_g
        else:
            of_ref[0] = y


def _in_proj(x, norm_w, scale, shift, w, nb, nf, gate_cols, tm=512):
    b, s, d = x.shape
    tm = min(tm, s)
    out_shape = [jax.ShapeDtypeStruct((b, s, nb), BF16)]
    out_specs = [pl.BlockSpec((1, tm, nb), lambda i, j: (i, j, 0))]
    if nf:
        out_shape.append(jax.ShapeDtypeStruct((b, s, nf), F32))
        out_specs.append(pl.BlockSpec((1, tm, nf), lambda i, j: (i, j, 0)))
    return pl.pallas_call(
        functools.partial(_in_proj_kernel, nb=nb, nf=nf, gate_cols=gate_cols),
        grid=(b, s // tm),
        in_specs=[pl.BlockSpec((1, tm, d), lambda i, j: (i, j, 0)),
                  pl.BlockSpec((1, d), lambda i, j: (0, 0)),
                  pl.BlockSpec((1, 1, d), lambda i, j: (i, 0, 0)),
                  pl.BlockSpec((1, 1, d), lambda i, j: (i, 0, 0)),
                  pl.BlockSpec((d, nb + nf), lambda i, j: (0, 0))],
        out_specs=out_specs,
        out_shape=out_shape,
        compiler_params=_cparams(("arbitrary", "arbitrary")),
        name="in_proj",
    )(x, norm_w.reshape(1, d), scale.reshape(b, 1, d), shift.reshape(b, 1, d), w)


def _out_proj_kernel(o1_ref, o2_ref, w1_ref, w2_ref, x_ref, nw_ref, gate_ref, xo_ref):
    y = _dot(o1_ref[0], w1_ref[...]) + _dot(o2_ref[0], w2_ref[...])
    xo_ref[0] = x_ref[0] + gate_ref[0] * _rms(y, nw_ref[...])


def _out_proj(o1, c1, o2, c2, w, x, norm_w, gate, tm=512):
    b, s, d = x.shape
    tm = min(tm, s)
    return pl.pallas_call(
        _out_proj_kernel,
        grid=(b, s // tm),
        in_specs=[pl.BlockSpec((1, tm, HQ), lambda i, j: (i, j, c1)),
                  pl.BlockSpec((1, tm, HQ), lambda i, j: (i, j, c2)),
                  pl.BlockSpec((HQ, d), lambda i, j: (0, 0)),
                  pl.BlockSpec((HQ, d), lambda i, j: (1, 0)),
                  pl.BlockSpec((1, tm, d), lambda i, j: (i, j, 0)),
                  pl.BlockSpec((1, d), lambda i, j: (0, 0)),
                  pl.BlockSpec((1, 1, d), lambda i, j: (i, 0, 0))],
        out_specs=pl.BlockSpec((1, tm, d), lambda i, j: (i, j, 0)),
        out_shape=jax.ShapeDtypeStruct((b, s, d), F32),
        compiler_params=_cparams(("arbitrary", "arbitrary")),
        name="out_proj",
    )(o1, o2, w, w, x, norm_w.reshape(1, d), gate.reshape(b, 1, d))


def _ffn_kernel(x_ref, nw1_ref, sc_ref, sh_ref, wg_ref, wu_ref, wo_ref, nw2_ref, gate_ref,
                xo_ref, h_ref, acc_ref):
    f = pl.program_id(2)

    @pl.when(f == 0)
    def _():
        h = _rms(x_ref[0], nw1_ref[...]) * (1.0 + sc_ref[0]) + sh_ref[0]
        h_ref[...] = h.astype(BF16)
        acc_ref[...] = jnp.zeros_like(acc_ref)

    hb = h_ref[...]
    g = _dot(hb, wg_ref[...])
    u = _dot(hb, wu_ref[...])
    a = (g * jax.nn.sigmoid(g) * u).astype(BF16)
    acc_ref[...] += _dot(a, wo_ref[...])

    @pl.when(f == pl.num_programs(2) - 1)
    def _():
        xo_ref[0] = x_ref[0] + gate_ref[0] * _rms(acc_ref[...], nw2_ref[...])


def _ffn(x, nw1, scale, shift, w_in, w_out, nw2, gate, tm=512, tf=1408):
    b, s, d = x.shape
    dff = w_out.shape[0]
    tm = min(tm, s)
    nf = dff // tf
    vec = lambda a: a.reshape(b, 1, d)
    return pl.pallas_call(
        _ffn_kernel,
        grid=(b, s // tm, nf),
        in_specs=[pl.BlockSpec((1, tm, d), lambda i, j, f: (i, j, 0)),
                  pl.BlockSpec((1, d), lambda i, j, f: (0, 0)),
                  pl.BlockSpec((1, 1, d), lambda i, j, f: (i, 0, 0)),
                  pl.BlockSpec((1, 1, d), lambda i, j, f: (i, 0, 0)),
                  pl.BlockSpec((d, tf), lambda i, j, f: (0, f)),
                  pl.BlockSpec((d, tf), lambda i, j, f: (0, nf + f)),
                  pl.BlockSpec((tf, d), lambda i, j, f: (f, 0)),
                  pl.BlockSpec((1, d), lambda i, j, f: (0, 0)),
                  pl.BlockSpec((1, 1, d), lambda i, j, f: (i, 0, 0))],
        out_specs=pl.BlockSpec((1, tm, d), lambda i, j, f: (i, j, 0)),
        out_shape=jax.ShapeDtypeStruct((b, s, d), F32),
        scratch_shapes=[pltpu.VMEM((tm, d), BF16), pltpu.VMEM((tm, d), F32)],
        compiler_params=_cparams(("arbitrary", "arbitrary", "arbitrary")),
        name="ffn",
    )(x, nw1.reshape(1, d), vec(scale), vec(shift), w_in, w_in, w_out, nw2.reshape(1, d), vec(gate))


def _rel_bucket(dist):
    n = jnp.maximum(dist, 0)
    max_exact = REL_BUCKETS // 2
    nf = jnp.maximum(n, 1).astype(jnp.float32)
    large = max_exact + (jnp.log(nf / max_exact) / math.log(REL_MAX_DIST / max_exact)
                         * (REL_BUCKETS - max_exact)).astype(jnp.int32)
    large = jnp.minimum(large, REL_BUCKETS - 1)
    return jnp.where(n < max_exact, n, large)


def _bias_table(tab, dist, valid):
    far = tab[REL_BUCKETS - 1]
    t = tab[_rel_bucket(dist)] - far
    t = jnp.where(valid[..., None], t, NEG).astype(F32)
    nq, nk = dist.shape
    return t.transpose(2, 0, 1).reshape(KV_HEADS, GROUP * nq, nk)


def _toeplitz(nq, nk, offset):
    return (jnp.arange(nq)[:, None] + offset) - jnp.arange(nk)[None, :]


def _compress_kernel(k_ref, w_ref, pe_ref, o_ref):
    nc = o_ref.shape[2]
    half = NSA_CMP_LEN // 2
    lo = jnp.zeros((nc, LANES), F32)
    hi = jnp.zeros((nc, LANES), F32)
    for l in range(half):
        rows = k_ref[0, pl.ds(l, nc, stride=NSA_CMP_STRIDE), :]
        lo = lo + _dot((rows + pe_ref[l:l + 1, :]).astype(BF16), w_ref[0, l])
        hi = hi + _dot((rows + pe_ref[half + l:half + l + 1, :]).astype(BF16), w_ref[0, half + l])
    o_ref[0, 0] = (lo + pltpu.roll(hi, nc - 1, axis=0)).astype(BF16)


def _compress(f32_proj, wk, wv, pe):
    b, s, _ = f32_proj.shape
    nc = s // NSA_CMP_STRIDE
    eye = jnp.eye(KV_HEADS, dtype=F32)
    bd = lambda w: jnp.einsum('gh,lde->lgdhe', eye, w).reshape(NSA_CMP_LEN, LANES, LANES)
    w = jnp.stack([bd(wk), bd(wv)]).astype(BF16)
    pe2 = jnp.tile(pe, (1, KV_HEADS)).astype(F32)
    return pl.pallas_call(
        _compress_kernel,
        grid=(b, 2),
        in_specs=[pl.BlockSpec((1, s, LANES), lambda i, j: (i, 0, j)),
                  pl.BlockSpec((1, NSA_CMP_LEN, LANES, LANES), lambda i, j: (j, 0, 0, 0)),
                  pl.BlockSpec((NSA_CMP_LEN, LANES), lambda i, j: (0, 0))],
        out_specs=pl.BlockSpec((1, 1, nc, LANES), lambda i, j: (i, j, 0, 0)),
        out_shape=jax.ShapeDtypeStruct((b, 2, nc, LANES), BF16),
        compiler_params=_cparams(("arbitrary", "arbitrary")),
        name="nsa_compress",
    )(f32_proj, w, pe2)


def _flash_init(m_ref, l_ref, acc_ref):
    m_ref[...] = jnp.full_like(m_ref, NEG)
    l_ref[...] = jnp.zeros_like(l_ref)
    acc_ref[...] = jnp.zeros_like(acc_ref)


def _flash_step(s, v, m_ref, l_ref, acc_ref):
    m_prev = m_ref[...]
    m_new = jnp.maximum(m_prev, jnp.max(s, axis=1, keepdims=True))
    alpha = jnp.exp(m_prev - m_new)
    p = jnp.exp(s - m_new)
    l_ref[...] = alpha * l_ref[...] + jnp.sum(p, axis=1, keepdims=True)
    acc_ref[...] = alpha * acc_ref[...] + _dot(p.astype(BF16), v)
    m_ref[...] = m_new


def _head_slab(q_ref, h, g, nq):
    slab = q_ref[0, :, LANES * (h // 2):LANES * (h // 2) + LANES].astype(F32) * SCALE
    if (h % 2) != g:
        slab = pltpu.roll(slab, HEAD_DIM, axis=1)
    return slab


def _stack_heads(q_ref, g, nq, other):
    lane = lax.broadcasted_iota(jnp.int32, (nq, LANES), 1)
    in_data = (lane // HEAD_DIM) == g
    return jnp.concatenate(
        [jnp.where(in_data, _head_slab(q_ref, GROUP * g + r, g, nq), other) for r in range(GROUP)],
        axis=0).astype(BF16)


def _unstack_heads(o, g, nq):
    lane = lax.broadcasted_iota(jnp.int32, (nq, LANES), 1)
    slabs = []
    for m in range(GROUP // 2):
        even = o[(2 * m) * nq:(2 * m + 1) * nq]
        odd = o[(2 * m + 1) * nq:(2 * m + 2) * nq]
        if g == 0:
            odd = pltpu.roll(odd, HEAD_DIM, axis=1)
        else:
            even = pltpu.roll(even, HEAD_DIM, axis=1)
        slabs.append(jnp.where(lane < HEAD_DIM, even, odd))
    return jnp.concatenate(slabs, axis=1)


def _build_kaug(k_ref, kaug_ref, block_len):
    s = k_ref.shape[1]
    ch = min(512, s)
    for g in range(KV_HEADS):
        for c in range(s // ch):
            lane = lax.broadcasted_iota(jnp.int32, (ch, LANES), 1)
            key = lax.broadcasted_iota(jnp.int32, (ch, LANES), 0) + c * ch
            in_data = (lane // HEAD_DIM) == g
            onehot = lane == (HEAD_DIM * (1 - g) + key // block_len)
            kaug_ref[g, c * ch:(c + 1) * ch, :] = jnp.where(
                in_data, k_ref[0, c * ch:(c + 1) * ch, :], onehot.astype(BF16))


def _pick_top(work, lane, count):
    picked = jnp.zeros(work.shape, jnp.bool_)
    picked_real = jnp.zeros(work.shape, jnp.bool_)
    lane_f = lane.astype(F32)
    for _ in range(count):
        mx = jnp.max(work, axis=1, keepdims=True)
        idx = jnp.min(jnp.where(work == mx, lane_f, float(LANES)), axis=1, keepdims=True)
        pick = lane_f == idx
        picked = picked | pick
        picked_real = picked_real | (pick & (mx > 0.5 * NEG))
        work = jnp.where(pick, BELOW_NEG, work)
    return picked, picked_real


def _nsa_swa_kernel(qa_ref, qb_ref, ksel_ref, vsel_ref, kwin_ref, vwin_ref, kb_ref, vb_ref,
                    gates_ref, kc_ref, vc_ref, bc_ref, tda_ref, tpa_ref, tdb_ref, tpb_ref,
                    ovw_ref, sink_ref, o_ref, kaug_ref, m_ref, l_ref, acc_ref):
    qi = pl.program_id(1)
    nq = Q_BLOCK
    rows = GROUP * nq

    @pl.when(qi == 0)
    def _():
        _build_kaug(ksel_ref, kaug_ref, NSA_SEL_LEN)

    lane = lax.broadcasted_iota(jnp.int32, (nq, LANES), 1)
    ql = lax.broadcasted_iota(jnp.int32, (nq, LANES), 0)
    row_q = lax.broadcasted_iota(jnp.int32, (rows, LANES), 0) % nq
    col = lax.broadcasted_iota(jnp.int32, (rows, LANES), 1)
    gates = gates_ref[0]

    def tile(ref, j):
        return ref[0, pl.ds(pl.multiple_of(j * nq, nq), nq), :]

    def finish():
        return acc_ref[...] / l_ref[...]

    for g in range(KV_HEADS):
        qz = _stack_heads(qa_ref, g, nq, 0.0)

        s = _dot_nt(qz, kc_ref[0, 0]) + bc_ref[g, 0]
        m = jnp.max(s, axis=1, keepdims=True)
        e = jnp.exp(s - m)
        p = jnp.where(s > 0.5 * NEG, e, 0.0) / jnp.sum(e, axis=1, keepdims=True)
        o_cmp = _dot(p.astype(BF16), vc_ref[0, 0])

        p4 = p[0:nq] + p[nq:2 * nq] + p[2 * nq:3 * nq] + p[3 * nq:4 * nq]
        p_hi, p_lo = _split2(p4)
        imp = _dot(p_hi, ovw_ref[g]) + _dot(p_lo, ovw_ref[g])
        j = lane - HEAD_DIM * (1 - g)
        own = 2 * qi + (ql >= NSA_SEL_LEN).astype(jnp.int32)
        forced = (j == 0) | (j == own) | (j == own - 1)
        imp = jnp.where(j <= own, imp + jnp.where(forced, NSA_FORCE, 0.0), NEG)
        work = jnp.where((lane // HEAD_DIM) == (1 - g), imp, BELOW_NEG)
        picked, _ = _pick_top(work, lane, NSA_SEL_TOP)
        selneg = jnp.where(picked, 0.0, NEG)
        q_aug = _stack_heads(qa_ref, g, nq, selneg)

        _flash_init(m_ref, l_ref, acc_ref)

        def far_step(t, carry):
            _flash_step(_dot_nt(q_aug, kaug_ref[g, pl.ds(pl.multiple_of(t * nq, nq), nq), :]),
                        tile(vsel_ref, t), m_ref, l_ref, acc_ref)
            return carry

        lax.fori_loop(0, jnp.maximum(qi - 1, 0), far_step, 0)

        @pl.when(qi >= 1)
        def _():
            k = kaug_ref[g, pl.ds(pl.multiple_of((qi - 1) * nq, nq), nq), :]
            _flash_step(_dot_nt(q_aug, k) + tpa_ref[g], tile(vsel_ref, qi - 1), m_ref, l_ref, acc_ref)

        k = kaug_ref[g, pl.ds(pl.multiple_of(qi * nq, nq), nq), :]
        _flash_step(_dot_nt(q_aug, k) + tda_ref[g], tile(vsel_ref, qi), m_ref, l_ref, acc_ref)
        o_sel = finish()

        _flash_init(m_ref, l_ref, acc_ref)
        n_back = NSA_WINDOW // nq
        for back in range(n_back, -1, -1):
            def win_tile(back=back):
                s = _dot_nt(qz, tile(kwin_ref, qi - back))
                if back == n_back:
                    s = jnp.where(col > row_q, s, NEG)
                elif back == 1:
                    s = s + tpa_ref[g]
                elif back == 0:
                    s = s + tda_ref[g]
                _flash_step(s, tile(vwin_ref, qi - back), m_ref, l_ref, acc_ref)
            if back == 0:
                win_tile()
            else:
                pl.when(qi >= back)(win_tile)
        o_win = finish()

        def gate_col(branch):
            cols = [jnp.broadcast_to(gates[:, 3 * (GROUP * g + r) + branch:3 * (GROUP * g + r) + branch + 1],
                                     (nq, LANES)) for r in range(GROUP)]
            return jnp.concatenate(cols, axis=0)

        o_a = gate_col(0) * o_cmp + gate_col(1) * o_sel + gate_col(2) * o_win
        o_ref[0, :, g * GROUP * HEAD_DIM:(g + 1) * GROUP * HEAD_DIM] = _unstack_heads(o_a, g, nq).astype(BF16)

        qz_b = _stack_heads(qb_ref, g, nq, 0.0)
        _flash_init(m_ref, l_ref, acc_ref)

        @pl.when(qi >= 1)
        def _():
            s = _dot_nt(qz_b, tile(kb_ref, qi - 1)) + tpb_ref[g]
            _flash_step(s, tile(vb_ref, qi - 1), m_ref, l_ref, acc_ref)

        _flash_step(_dot_nt(qz_b, tile(kb_ref, qi)) + tdb_ref[g], tile(vb_ref, qi), m_ref, l_ref, acc_ref)
        sink = sink_ref[g]
        m_f = m_ref[...]
        m2 = jnp.maximum(m_f, sink)
        a = jnp.exp(m_f - m2)
        o_b = (a * acc_ref[...]) / (a * l_ref[...] + jnp.exp(sink - m2))
        o_ref[0, :, HQ + g * GROUP * HEAD_DIM:HQ + (g + 1) * GROUP * HEAD_DIM] = (
            _unstack_heads(o_b, g, nq).astype(BF16))


def _nsa_swa(pb, pf, kvc, rel_table, sinks):
    b, s, _ = pb.shape
    nq = Q_BLOCK
    n_blk = s // nq
    nc = s // NSA_CMP_STRIDE
    n_cmp = (s - NSA_CMP_LEN) // NSA_CMP_STRIDE + 1
    n_sel = s // NSA_SEL_LEN
    assert n_sel <= HEAD_DIM and s % nq == 0 and s >= 2 * nq
    tab_a = rel_table[:, :HEADS_PER_MIXER]
    tab_b = rel_table[:, HEADS_PER_MIXER:2 * HEADS_PER_MIXER]

    tq = jnp.arange(s)
    cmp_end = jnp.arange(nc) * NSA_CMP_STRIDE + NSA_CMP_LEN - 1
    dist = tq[:, None] - cmp_end[None, :]
    valid = (dist >= 0) & (jnp.arange(nc)[None, :] < n_cmp)
    far = tab_a[REL_BUCKETS - 1]
    bc = jnp.where(valid[..., None], tab_a[_rel_bucket(dist)] - far, NEG).astype(F32)
    bc = bc.reshape(n_blk, nq, nc, KV_HEADS, GROUP).transpose(3, 0, 4, 1, 2).reshape(
        KV_HEADS, n_blk, GROUP * nq, nc)

    d_diag = _toeplitz(nq, nq, 0)
    d_prev = _toeplitz(nq, nq, nq)
    tda = _bias_table(tab_a, d_diag, d_diag >= 0)
    tpa = _bias_table(tab_a, d_prev, d_prev >= 0)
    tdb = _bias_table(tab_b, d_diag, d_diag >= 0)
    tpb = _bias_table(tab_b, d_prev, d_prev < SWA_WINDOW)

    cmp_start = jnp.arange(nc) * NSA_CMP_STRIDE
    sel_start = jnp.arange(n_sel) * NSA_SEL_LEN
    overlap = jnp.maximum(jnp.minimum(cmp_start[:, None] + NSA_CMP_LEN, sel_start[None, :] + NSA_SEL_LEN)
                          - jnp.maximum(cmp_start[:, None], sel_start[None, :]), 0)
    overlap = jnp.where(jnp.arange(nc)[:, None] < n_cmp, overlap, 0).astype(F32) / NSA_CMP_LEN
    ovw = jnp.zeros((KV_HEADS, nc, LANES), F32)
    ovw = ovw.at[0, :, HEAD_DIM:HEAD_DIM + n_sel].set(overlap).at[1, :, :n_sel].set(overlap)
    ovw = ovw.astype(BF16)

    sink = (sinks.astype(F32) - tab_b[REL_BUCKETS - 1]).reshape(KV_HEADS, GROUP, 1)
    sink = jnp.broadcast_to(sink, (KV_HEADS, GROUP, nq)).reshape(KV_HEADS, GROUP * nq, 1)

    rows = GROUP * nq
    kv = lambda c: pl.BlockSpec((1, s, LANES), lambda i, j: (i, 0, c))
    whole = lambda a: pl.BlockSpec(a.shape, lambda i, j: (0,) * a.ndim)
    return pl.pallas_call(
        _nsa_swa_kernel,
        grid=(b, n_blk),
        in_specs=[pl.BlockSpec((1, nq, HQ), lambda i, j: (i, j, 0)),
                  pl.BlockSpec((1, nq, HQ), lambda i, j: (i, j, 1)),
                  kv(8), kv(9), kv(10), kv(11), kv(12), kv(13),
                  pl.BlockSpec((1, nq, LANES), lambda i, j: (i, j, 2)),
                  pl.BlockSpec((1, 1, nc, LANES), lambda i, j: (i, 0, 0, 0)),
                  pl.BlockSpec((1, 1, nc, LANES), lambda i, j: (i, 1, 0, 0)),
                  pl.BlockSpec((KV_HEADS, 1, rows, nc), lambda i, j: (0, j, 0, 0)),
                  whole(tda), whole(tpa), whole(tdb), whole(tpb), whole(ovw), whole(sink)],
        out_specs=pl.BlockSpec((1, nq, 2 * HQ), lambda i, j: (i, j, 0)),
        out_shape=jax.ShapeDtypeStruct((b, s, 2 * HQ), BF16),
        scratch_shapes=[pltpu.VMEM((KV_HEADS, s, LANES), BF16),
                        pltpu.VMEM((rows, 1), F32), pltpu.VMEM((rows, 1), F32),
                        pltpu.VMEM((rows, LANES), F32)],
        compiler_params=_cparams(("arbitrary", "arbitrary")),
        name="nsa_swa",
    )(pb, pb, pb, pb, pb, pb, pb, pb, pf, kvc, kvc, bc, tda, tpa, tdb, tpb, ovw, sink)


def _moba_kernel(q_ref, k_ref, v_ref, td_ref, tp_ref, o_ref, kaug_ref, kmx_ref, m_ref, l_ref, acc_ref):
    qi = pl.program_id(1)
    nq = MOBA_BLOCK
    rows = GROUP * nq
    s_len = k_ref.shape[1]
    nb = s_len // nq

    @pl.when(qi == 0)
    def _():
        _build_kaug(k_ref, kaug_ref, MOBA_BLOCK)
        kmx_ref[...] = jnp.zeros_like(kmx_ref)
        lane = lax.broadcasted_iota(jnp.int32, (1, LANES), 1)
        for blk in range(nb):
            mean = jnp.mean(k_ref[0, blk * nq:(blk + 1) * nq, :].astype(F32), axis=0, keepdims=True)
            for g in range(KV_HEADS):
                hi, lo = _split2(jnp.where((lane // HEAD_DIM) == g, mean, 0.0))
                r = HEAD_DIM * (1 - g) + blk
                kmx_ref[0, g, r:r + 1, :] = hi
                kmx_ref[1, g, r:r + 1, :] = lo

    lane = lax.broadcasted_iota(jnp.int32, (rows, LANES), 1)

    def tile(ref, j):
        return ref[0, pl.ds(pl.multiple_of(j * nq, nq), nq), :]

    for g in range(KV_HEADS):
        qz = _stack_heads(q_ref, g, nq, 0.0)
        gs = _dot_nt(qz, kmx_ref[0, g]) + _dot_nt(qz, kmx_ref[1, g])
        j = lane - HEAD_DIM * (1 - g)
        work = jnp.where((j >= 0) & (j < qi), gs, BELOW_NEG)
        _, picked = _pick_top(work, lane, MOBA_TOP)
        selneg = jnp.where(picked, 0.0, NEG)
        in_data = (lane // HEAD_DIM) == g
        q_aug = jnp.where(in_data, qz, selneg.astype(BF16))

        _flash_init(m_ref, l_ref, acc_ref)

        def far_step(t, carry):
            _flash_step(_dot_nt(q_aug, kaug_ref[g, pl.ds(pl.multiple_of(t * nq, nq), nq), :]),
                        tile(v_ref, t), m_ref, l_ref, acc_ref)
            return carry

        lax.fori_loop(0, jnp.maximum(qi - 1, 0), far_step, 0)

        @pl.when(qi >= 1)
        def _():
            k = kaug_ref[g, pl.ds(pl.multiple_of((qi - 1) * nq, nq), nq), :]
            _flash_step(_dot_nt(q_aug, k) + tp_ref[g], tile(v_ref, qi - 1), m_ref, l_ref, acc_ref)

        _flash_step(_dot_nt(qz, tile(k_ref, qi)) + td_ref[g], tile(v_ref, qi), m_ref, l_ref, acc_ref)
        o = acc_ref[...] / l_ref[...]
        o_ref[0, :, g * GROUP * HEAD_DIM:(g + 1) * GROUP * HEAD_DIM] = _unstack_heads(o, g, nq).astype(BF16)


def _moba(pb, rel_table, q_col, k_col, v_col):
    b, s, _ = pb.shape
    nq = MOBA_BLOCK
    nb = s // nq
    assert s % nq == 0 and nb >= 2 and nb <= HEAD_DIM
    tab = rel_table[:, :HEADS_PER_MIXER]
    d_diag = _toeplitz(nq, nq, 0)
    d_prev = _toeplitz(nq, nq, nq)
    td = _bias_table(tab, d_diag, d_diag >= 0)
    tp = _bias_table(tab, d_prev, d_prev >= 0)
    rows = GROUP * nq
    whole = lambda a: pl.BlockSpec(a.shape, lambda i, j: (0,) * a.ndim)
    return pl.pallas_call(
        _moba_kernel,
        grid=(b, nb),
        in_specs=[pl.BlockSpec((1, nq, HQ), lambda i, j: (i, j, q_col)),
                  pl.BlockSpec((1, s, LANES), lambda i, j: (i, 0, k_col)),
                  pl.BlockSpec((1, s, LANES), lambda i, j: (i, 0, v_col)),
                  whole(td), whole(tp)],
        out_specs=pl.BlockSpec((1, nq, HQ), lambda i, j: (i, j, 0)),
        out_shape=jax.ShapeDtypeStruct((b, s, HQ), BF16),
        scratch_shapes=[pltpu.VMEM((KV_HEADS, s, LANES), BF16),
                        pltpu.VMEM((2, KV_HEADS, LANES, LANES), BF16),
                        pltpu.VMEM((rows, 1), F32), pltpu.VMEM((rows, 1), F32),
                        pltpu.VMEM((rows, LANES), F32)],
        compiler_params=_cparams(("arbitrary", "arbitrary")),
        name="moba",
    )(pb, pb, pb, td, tp)


def _stick_kernel(q_ref, k_ref, v_ref, o_ref, acc_ref, c_ref):
    qi = pl.program_id(2)
    tq, tk = SB_TQ, SB_TK
    lane = lax.broadcasted_iota(jnp.int32, (tq, LANES), 1)
    q = q_ref[0].astype(F32) * SCALE
    qh = [jnp.where((lane // HEAD_DIM) == h, q, 0.0).astype(BF16) for h in range(2)]
    tri = (lax.broadcasted_iota(jnp.int32, (tk, tk), 0) > lax.broadcasted_iota(jnp.int32, (tk, tk), 1)).astype(BF16)
    tpos = qi * tq + lax.broadcasted_iota(jnp.int32, (tq, tk), 0)
    kloc = lax.broadcasted_iota(jnp.int32, (tq, tk), 1)

    acc_ref[...] = jnp.zeros_like(acc_ref)
    c_ref[...] = jnp.zeros_like(c_ref)

    def step(i, carry):
        t = (qi + 1) * (tq // tk) - 1 - i
        rows = pl.ds(pl.multiple_of(t * tk, tk), tk)
        k = k_ref[0, rows, :]
        v = v_ref[0, rows, :]
        strict = (kloc + t * tk) < tpos
        for h in range(2):
            z = _dot_nt(qh[h], k)
            log_1m = -(jnp.maximum(z, 0.0) + jnp.log1p(jnp.exp(-jnp.abs(z))))
            log_1m = jnp.where(strict, log_1m, 0.0)
            hi, lo = _split2(log_1m)
            after = _dot(hi, tri) + _dot(lo, tri) + c_ref[h]
            w = jnp.where(strict, jnp.exp(log_1m + z + after), 0.0)
            acc_ref[h] += _dot(w.astype(BF16), v)
            c_ref[h] += jnp.sum(log_1m, axis=1, keepdims=True)
        return carry

    lax.fori_loop(0, (qi + 1) * (tq // tk), step, 0)
    o_ref[0] = jnp.where((lane // HEAD_DIM) == 0, acc_ref[0], acc_ref[1]).astype(BF16)


def _stick_breaking(pb, q_col, k_col, v_col):
    b, s, _ = pb.shape
    tq = min(SB_TQ, s)
    assert s % tq == 0 and tq == SB_TQ
    pairs = HEADS_PER_MIXER // 2
    return pl.pallas_call(
        _stick_kernel,
        grid=(b, pairs, s // tq),
        in_specs=[pl.BlockSpec((1, tq, LANES), lambda i, p, j: (i, j, q_col + p)),
                  pl.BlockSpec((1, s, LANES), lambda i, p, j: (i, 0, k_col + p)),
                  pl.BlockSpec((1, s, LANES), lambda i, p, j: (i, 0, v_col + p))],
        out_specs=pl.BlockSpec((1, tq, LANES), lambda i, p, j: (i, j, p)),
        out_shape=jax.ShapeDtypeStruct((b, s, HQ), BF16),
        scratch_shapes=[pltpu.VMEM((2, tq, LANES), F32), pltpu.VMEM((2, tq, 1), F32)],
        compiler_params=_cparams(("arbitrary", "arbitrary", "arbitrary")),
        name="stick_breaking",
    )(pb, pb, pb)


def _split_cols(w, sizes):
    offs = [0]
    for z in sizes:
        offs.append(offs[-1] + z)
    return [w[:, offs[i]:offs[i + 1]] for i in range(len(sizes))]


def kernel(x, c, rel_table, mod_w, mod_b, norm_w, w_in_ab, w_out_ab, nsa_cmp_wk, nsa_cmp_wv,
           nsa_cmp_pe, swa_sinks, w_in_cd, w_out_cd, ffn_w_in, ffn_w_out):
    depth = mod_w.shape[0]
    d = x.shape[-1]
    h8 = HEADS_PER_MIXER
    mod = _modulation(c, mod_w.reshape(depth * 2, d, 3 * d), mod_b.reshape(depth * 2, 3 * d))
    mod = mod.reshape(depth, 2, c.shape[0], 3, d)

    for layer in range(depth):
        shift, scale, gate = mod[layer, 0, :, 0], mod[layer, 0, :, 1], mod[layer, 0, :, 2]
        i = layer // 2
        if layer % 2 == 0:
            qa, kca, vca, ksa, vsa, kwa, vwa, ga, qb, kb, vb = _split_cols(
                w_in_ab[i], [HQ, KVW, KVW, KVW, KVW, KVW, KVW, 3 * h8, HQ, KVW, KVW])
            ga = jnp.pad(ga, ((0, 0), (0, LANES - 3 * h8)))
            w = jnp.concatenate([qa, qb, ksa, vsa, kwa, vwa, kb, vb, kca, vca, ga], axis=1).astype(BF16)
            nb = 2 * HQ + 6 * KVW
            pb, pf = _in_proj(x, norm_w[layer, 0, 0], scale, shift, w, nb, 3 * LANES, LANES)
            kvc = _compress(pf, nsa_cmp_wk[i], nsa_cmp_wv[i], nsa_cmp_pe[i])
            o = _nsa_swa(pb, pf, kvc, rel_table, swa_sinks[i])
            x = _out_proj(o, 0, o, 1, w_out_ab[i].astype(BF16), x, norm_w[layer, 0, 1], gate)
        else:
            qc, kc, vc, qd, kd, vd = _split_cols(w_in_cd[i], [HQ, KVW, KVW, HQ, HQ, HQ])
            w = jnp.concatenate([qc, qd, kd, vd, kc, vc], axis=1).astype(BF16)
            (pb,) = _in_proj(x, norm_w[layer, 0, 0], scale, shift, w, 4 * HQ + 2 * KVW, 0, 0)
            o_c = _moba(pb, rel_table, 0, 4 * HQ // LANES, 4 * HQ // LANES + 1)
            o_d = _stick_breaking(pb, HQ // LANES, 2 * HQ // LANES, 3 * HQ // LANES)
            x = _out_proj(o_c, 0, o_d, 0, w_out_cd[i].astype(BF16), x, norm_w[layer, 0, 1], gate)

        shift, scale, gate = mod[layer, 1, :, 0], mod[layer, 1, :, 1], mod[layer, 1, :, 2]
        x = _ffn(x, norm_w[layer, 1, 0], scale, shift, ffn_w_in[layer].astype(BF16),
                 ffn_w_out[layer].astype(BF16), norm_w[layer, 1, 1], gate)
    return x
```

```python
import functools
import math

import jax
import jax.numpy as jnp
from jax import lax
from jax.experimental import pallas as pl
from jax.experimental.pallas import tpu as pltpu

F32 = jnp.float32
BF16 = jnp.bfloat16

HEAD_DIM = 64
LANES = 128
KV_HEADS = 2
GROUP = 4
HEADS_PER_MIXER = KV_HEADS * GROUP
HQ = HEADS_PER_MIXER * HEAD_DIM
KVW = KV_HEADS * HEAD_DIM
Q_BLOCK = 128
NSA_CMP_LEN = 32
NSA_CMP_STRIDE = 16
NSA_SEL_LEN = 64
NSA_SEL_TOP = 8
NSA_WINDOW = 512
NSA_FORCE = 1e4
SWA_WINDOW = 128
MOBA_BLOCK = 256
MOBA_TOP = 3
REL_BUCKETS = 32
REL_MAX_DIST = 128
RMS_EPS = 1e-6
NEG = -1e30
MASK = -(2.0 ** 100)
BELOW_NEG = -3e38
SCALE = HEAD_DIM ** -0.5
FAR_TILE = 512
SB_T = 256
EXP_ZERO = -104.0
VMEM_LIMIT = 56 * 1024 * 1024

_NT = (((1,), (1,)), ((), ()))


def _cparams(sem):
    return pltpu.CompilerParams(dimension_semantics=sem, vmem_limit_bytes=VMEM_LIMIT)


def _dot(a, b):
    return jnp.dot(a, b, preferred_element_type=F32)


def _dot_nt(a, b):
    return lax.dot_general(a, b, _NT, preferred_element_type=F32)


def _split2(x):
    hi = x.astype(BF16)
    lo = (x - hi.astype(F32)).astype(BF16)
    return hi, lo


def _rms(x, w):
    return x * lax.rsqrt(jnp.mean(x * x, axis=-1, keepdims=True) + RMS_EPS) * w


def _mod_kernel(c_ref, w_ref, b_ref, o_ref):
    o_ref[0] = jnp.dot(c_ref[...], w_ref[0], preferred_element_type=F32,
                       precision=lax.Precision.HIGHEST) + b_ref[0]


def _modulation(c, mod_w, mod_b):
    n, d, d3 = mod_w.shape
    b = c.shape[0]
    tn = 1024
    return pl.pallas_call(
        _mod_kernel,
        grid=(n, d3 // tn),
        in_specs=[pl.BlockSpec((b, d), lambda i, j: (0, 0)),
                  pl.BlockSpec((1, d, tn), lambda i, j: (i, 0, j)),
                  pl.BlockSpec((1, 1, tn), lambda i, j: (i, 0, j))],
        out_specs=pl.BlockSpec((1, b, tn), lambda i, j: (i, 0, j)),
        out_shape=jax.ShapeDtypeStruct((n, b, d3), F32),
        compiler_params=_cparams(("arbitrary", "arbitrary")),
        name="modulation",
    )(c, mod_w, mod_b.reshape(n, 1, d3))


def _in_proj_kernel(x_ref, nw_ref, sc_ref, sh_ref, w_ref, *out_refs, nb, nf, gate_cols):
    x = x_ref[0]
    h = _rms(x, nw_ref[...]) * (1.0 + sc_ref[0]) + sh_ref[0]
    hb = h.astype(BF16)
    ob_ref = out_refs[0]
    for c0 in range(0, nb, 512):
        c1 = min(c0 + 512, nb)
        ob_ref[0, :, c0:c1] = _dot(hb, w_ref[:, c0:c1]).astype(BF16)
    if nf:
        of_ref = out_refs[1]
        y = _dot(hb, w_ref[:, nb:nb + nf])
        if gate_cols:
            y_g = jax.nn.sigmoid(y[:, nf - gate_cols:])
            of_ref[0, :, :nf - gate_cols] = y[:, :nf - gate_cols]
            of_ref[0, :, nf - gate_cols:] = y_g
        else:
            of_ref[0] = y


def _in_proj(x, norm_w, scale, shift, w, nb, nf, gate_cols, tm=512):
    b, s, d = x.shape
    tm = min(tm, s)
    out_shape = [jax.ShapeDtypeStruct((b, s, nb), BF16)]
    out_specs = [pl.BlockSpec((1, tm, nb), lambda i, j: (i, j, 0))]
    if nf:
        out_shape.append(jax.ShapeDtypeStruct((b, s, nf), F32))
        out_specs.append(pl.BlockSpec((1, tm, nf), lambda i, j: (i, j, 0)))
    return pl.pallas_call(
        functools.partial(_in_proj_kernel, nb=nb, nf=nf, gate_cols=gate_cols),
        grid=(b, s // tm),
        in_specs=[pl.BlockSpec((1, tm, d), lambda i, j: (i, j, 0)),
                  pl.BlockSpec((1, d), lambda i, j: (0, 0)),
                  pl.BlockSpec((1, 1, d), lambda i, j: (i, 0, 0)),
                  pl.BlockSpec((1, 1, d), lambda i, j: (i, 0, 0)),
                  pl.BlockSpec((d, nb + nf), lambda i, j: (0, 0))],
        out_specs=out_specs,
        out_shape=out_shape,
        compiler_params=_cparams(("arbitrary", "arbitrary")),
        name="in_proj",
    )(x, norm_w.reshape(1, d), scale.reshape(b, 1, d), shift.reshape(b, 1, d), w)


def _out_proj_kernel(o1_ref, o2_ref, w1_ref, w2_ref, x_ref, nw_ref, gate_ref, xo_ref):
    y = _dot(o1_ref[0], w1_ref[...]) + _dot(o2_ref[0], w2_ref[...])
    xo_ref[0] = x_ref[0] + gate_ref[0] * _rms(y, nw_ref[...])


def _out_proj(o1, c1, o2, c2, w, x, norm_w, gate, tm=512):
    b, s, d = x.shape
    tm = min(tm, s)
    return pl.pallas_call(
        _out_proj_kernel,
        grid=(b, s // tm),
        in_specs=[pl.BlockSpec((1, tm, HQ), lambda i, j: (i, j, c1)),
                  pl.BlockSpec((1, tm, HQ), lambda i, j: (i, j, c2)),
                  pl.BlockSpec((HQ, d), lambda i, j: (0, 0)),
                  pl.BlockSpec((HQ, d), lambda i, j: (1, 0)),
                  pl.BlockSpec((1, tm, d), lambda i, j: (i, j, 0)),
                  pl.BlockSpec((1, d), lambda i, j: (0, 0)),
                  pl.BlockSpec((1, 1, d), lambda i, j: (i, 0, 0))],
        out_specs=pl.BlockSpec((1, tm, d), lambda i, j: (i, j, 0)),
        out_shape=jax.ShapeDtypeStruct((b, s, d), F32),
        compiler_params=_cparams(("arbitrary", "arbitrary")),
        name="out_proj",
    )(o1, o2, w, w, x, norm_w.reshape(1, d), gate.reshape(b, 1, d))


def _ffn_kernel(x_ref, nw1_ref, sc_ref, sh_ref, wg_ref, wu_ref, wo_ref, nw2_ref, gate_ref,
                xo_ref, h_ref, acc_ref):
    f = pl.program_id(2)

    @pl.when(f == 0)
    def _():
        h = _rms(x_ref[0], nw1_ref[...]) * (1.0 + sc_ref[0]) + sh_ref[0]
        h_ref[...] = h.astype(BF16)
        acc_ref[...] = jnp.zeros_like(acc_ref)

    hb = h_ref[...]
    g = _dot(hb, wg_ref[...])
    u = _dot(hb, wu_ref[...])
    a = (g * jax.nn.sigmoid(g) * u).astype(BF16)
    acc_ref[...] += _dot(a, wo_ref[...])

    @pl.when(f == pl.num_programs(2) - 1)
    def _():
        xo_ref[0] = x_ref[0] + gate_ref[0] * _rms(acc_ref[...], nw2_ref[...])


def _ffn(x, nw1, scale, shift, w_in, w_out, nw2, gate, tm=512, tf=1408):
    b, s, d = x.shape
    dff = w_out.shape[0]
    tm = min(tm, s)
    nf = dff // tf
    vec = lambda a: a.reshape(b, 1, d)
    return pl.pallas_call(
        _ffn_kernel,
        grid=(b, s // tm, nf),
        in_specs=[pl.BlockSpec((1, tm, d), lambda i, j, f: (i, j, 0)),
                  pl.BlockSpec((1, d), lambda i, j, f: (0, 0)),
                  pl.BlockSpec((1, 1, d), lambda i, j, f: (i, 0, 0)),
                  pl.BlockSpec((1, 1, d), lambda i, j, f: (i, 0, 0)),
                  pl.BlockSpec((d, tf), lambda i, j, f: (0, f)),
                  pl.BlockSpec((d, tf), lambda i, j, f: (0, nf + f)),
                  pl.BlockSpec((tf, d), lambda i, j, f: (f, 0)),
                  pl.BlockSpec((1, d), lambda i, j, f: (0, 0)),
                  pl.BlockSpec((1, 1, d), lambda i, j, f: (i, 0, 0))],
        out_specs=pl.BlockSpec((1, tm, d), lambda i, j, f: (i, j, 0)),
        out_shape=jax.ShapeDtypeStruct((b, s, d), F32),
        scratch_shapes=[pltpu.VMEM((tm, d), BF16), pltpu.VMEM((tm, d), F32)],
        compiler_params=_cparams(("arbitrary", "arbitrary", "arbitrary")),
        name="ffn",
    )(x, nw1.reshape(1, d), vec(scale), vec(shift), w_in, w_in, w_out, nw2.reshape(1, d), vec(gate))


def _rel_bucket(dist):
    n = jnp.maximum(dist, 0)
    max_exact = REL_BUCKETS // 2
    nf = jnp.maximum(n, 1).astype(jnp.float32)
    large = max_exact + (jnp.log(nf / max_exact) / math.log(REL_MAX_DIST / max_exact)
                         * (REL_BUCKETS - max_exact)).astype(jnp.int32)
    large = jnp.minimum(large, REL_BUCKETS - 1)
    return jnp.where(n < max_exact, n, large)


def _bias_by_dist(tab, n):
    return (tab[_rel_bucket(jnp.arange(n))] - tab[REL_BUCKETS - 1]).T.astype(F32)


def _toeplitz(f, nq, nk, offset):
    m = nq + nk - 1
    lo = offset - (nk - 1)
    idx = jnp.clip(jnp.arange(m) + lo, 0, f.shape[1] - 1)
    w = f[:, idx]
    flat = jnp.tile(w, (1, nq + 1))[:, :nq * (m + 1)]
    hankel = flat.reshape(f.shape[0], nq, m + 1)[:, :, :nk]
    return hankel[:, :, ::-1]


def _rows_by_group(t):
    h, nq, nk = t.shape
    return t.reshape(KV_HEADS, GROUP * nq, nk)


def _dist(nq, nk, offset):
    return (jnp.arange(nq)[:, None] + offset) - jnp.arange(nk)[None, :]


def _masked(t, valid):
    return jnp.where(valid[None], t, NEG)


def _compress_kernel(k_ref, w_ref, pe_ref, o_ref):
    nc = o_ref.shape[2]
    half = NSA_CMP_LEN // 2
    lo = jnp.zeros((nc, LANES), F32)
    hi = jnp.zeros((nc, LANES), F32)
    for l in range(half):
        rows = k_ref[0, pl.ds(l, nc, stride=NSA_CMP_STRIDE), :]
        lo = lo + _dot((rows + pe_ref[l:l + 1, :]).astype(BF16), w_ref[0, l])
        hi = hi + _dot((rows + pe_ref[half + l:half + l + 1, :]).astype(BF16), w_ref[0, half + l])
    o_ref[0, 0] = (lo + pltpu.roll(hi, nc - 1, axis=0)).astype(BF16)


def _compress(f32_proj, wk, wv, pe):
    b, s, _ = f32_proj.shape
    nc = s // NSA_CMP_STRIDE
    eye = jnp.eye(KV_HEADS, dtype=F32)
    bd = lambda w: jnp.einsum('gh,lde->lgdhe', eye, w).reshape(NSA_CMP_LEN, LANES, LANES)
    w = jnp.stack([bd(wk), bd(wv)]).astype(BF16)
    pe2 = jnp.tile(pe, (1, KV_HEADS)).astype(F32)
    return pl.pallas_call(
        _compress_kernel,
        grid=(b, 2),
        in_specs=[pl.BlockSpec((1, s, LANES), lambda i, j: (i, 0, j)),
                  pl.BlockSpec((1, NSA_CMP_LEN, LANES, LANES), lambda i, j: (j, 0, 0, 0)),
                  pl.BlockSpec((NSA_CMP_LEN, LANES), lambda i, j: (0, 0))],
        out_specs=pl.BlockSpec((1, 1, nc, LANES), lambda i, j: (i, j, 0, 0)),
        out_shape=jax.ShapeDtypeStruct((b, 2, nc, LANES), BF16),
        compiler_params=_cparams(("arbitrary", "arbitrary")),
        name="nsa_compress",
    )(f32_proj, w, pe2)


def _rowmax(s):
    mx = s[:, :LANES]
    for c in range(LANES, s.shape[1], LANES):
        mx = jnp.maximum(mx, s[:, c:c + LANES])
    return jnp.broadcast_to(jnp.max(mx, axis=1, keepdims=True), (s.shape[0], LANES))


def _exp_sub(s, m):
    return jnp.concatenate([jnp.exp(s[:, c:c + LANES] - m).astype(BF16)
                            for c in range(0, s.shape[1], LANES)], axis=1)


def _flash_step(s, v_aug, m_ref, acc_ref):
    m_prev = m_ref[...]
    m_new = jnp.maximum(m_prev, _rowmax(s))
    acc_ref[...] = jnp.exp(m_prev - m_new) * acc_ref[...] + _dot(_exp_sub(s, m_new), v_aug)
    m_ref[...] = m_new


def _normalize(acc):
    return acc / pltpu.roll(acc, HEAD_DIM, axis=1)


def _head_slab(q_ref, h, g):
    slab = q_ref[0, :, LANES * (h // 2):LANES * (h // 2) + LANES].astype(F32) * SCALE
    if (h % 2) != g:
        slab = pltpu.roll(slab, HEAD_DIM, axis=1)
    return slab


def _stack_heads(q_ref, g, nq):
    lane = lax.broadcasted_iota(jnp.int32, (nq, LANES), 1)
    in_data = (lane // HEAD_DIM) == g
    return jnp.concatenate(
        [jnp.where(in_data, _head_slab(q_ref, GROUP * g + r, g), 0.0) for r in range(GROUP)], axis=0)


def _unstack_heads(o, g, nq):
    lane = lax.broadcasted_iota(jnp.int32, (nq, LANES), 1)
    slabs = []
    for m in range(GROUP // 2):
        even = o[(2 * m) * nq:(2 * m + 1) * nq]
        odd = o[(2 * m + 1) * nq:(2 * m + 2) * nq]
        if g == 0:
            odd = pltpu.roll(odd, HEAD_DIM, axis=1)
        else:
            even = pltpu.roll(even, HEAD_DIM, axis=1)
        slabs.append(jnp.where(lane < HEAD_DIM, even, odd))
    return jnp.concatenate(slabs, axis=1)


def _fill_aug(src_ref, dst_ref, block_len, row0=0):
    s = src_ref.shape[1]
    ch = min(512, s)
    for g in range(KV_HEADS):
        for c in range(s // ch):
            lane = lax.broadcasted_iota(jnp.int32, (ch, LANES), 1)
            in_data = (lane // HEAD_DIM) == g
            if block_len:
                key = lax.broadcasted_iota(jnp.int32, (ch, LANES), 0) + c * ch
                aux = (lane == (HEAD_DIM * (1 - g) + key // block_len)).astype(BF16)
            else:
                aux = jnp.ones((ch, LANES), BF16)
            dst_ref[g, row0 + c * ch:row0 + (c + 1) * ch, :] = jnp.where(
                in_data, src_ref[0, c * ch:(c + 1) * ch, :], aux)


def _pick_top(work, lane, count):
    picked = jnp.zeros(work.shape, jnp.bool_)
    picked_real = jnp.zeros(work.shape, jnp.bool_)
    lane_f = lane.astype(F32)
    for _ in range(count):
        mx = jnp.max(work, axis=1, keepdims=True)
        idx = jnp.min(jnp.where(work == mx, lane_f, float(LANES)), axis=1, keepdims=True)
        pick = lane_f == idx
        picked = picked | pick
        picked_real = picked_real | (pick & (mx > 0.5 * NEG))
        work = jnp.where(pick, BELOW_NEG, work)
    return picked, picked_real


def _with_aux(qz, aux, g):
    lane = lax.broadcasted_iota(jnp.int32, qz.shape, 1)
    return jnp.where((lane // HEAD_DIM) == g, qz, aux).astype(BF16)


def _nsa_swa_kernel(qa_ref, qb_ref, ksel_ref, vsel_ref, kwin_ref, vwin_ref, kb_ref, vb_ref,
                    gates_ref, kc_ref, vc_ref, tc_ref, tnear_ref, twin_ref, tswa_ref,
                    ovw_ref, sink_ref, o_ref,
                    kaug_ref, vsel_aug_ref, kwin_pad_ref, vwin_aug_ref, vb_aug_ref, m_ref, acc_ref):
    qi = pl.program_id(1)
    nq = Q_BLOCK
    rows = GROUP * nq
    nc = kc_ref.shape[2]
    n_back = NSA_WINDOW // nq

    @pl.when(qi == 0)
    def _():
        _fill_aug(ksel_ref, kaug_ref, NSA_SEL_LEN)
        _fill_aug(vsel_ref, vsel_aug_ref, 0)
        _fill_aug(vb_ref, vb_aug_ref, 0)
        kwin_pad_ref[:NSA_WINDOW, :] = jnp.zeros((NSA_WINDOW, LANES), BF16)
        kwin_pad_ref[NSA_WINDOW:, :] = kwin_ref[0]
        vwin_aug_ref[:, :NSA_WINDOW, :] = jnp.zeros((KV_HEADS, NSA_WINDOW, LANES), BF16)
        _fill_aug(vwin_ref, vwin_aug_ref, 0, row0=NSA_WINDOW)

    lane = lax.broadcasted_iota(jnp.int32, (nq, LANES), 1)
    ql = lax.broadcasted_iota(jnp.int32, (nq, LANES), 0)
    gates = gates_ref[0]
    first = jnp.minimum(qi, 1)
    near0 = pl.multiple_of(jnp.maximum(qi - 1, 0) * nq, nq)

    u_id = lax.broadcasted_iota(jnp.int32, (LANES, nc), 0)
    n_id = lax.broadcasted_iota(jnp.int32, (LANES, nc), 1)
    per_blk = nq // NSA_CMP_STRIDE
    shift = ((u_id < 2 * per_blk) & (n_id == per_blk * qi - per_blk + u_id)).astype(BF16)
    cmp_future = lax.broadcasted_iota(jnp.int32, (rows, nc), 1) >= per_blk * qi + per_blk

    for g in range(KV_HEADS):
        qz = _stack_heads(qa_ref, g, nq)
        qz_b16 = qz.astype(BF16)

        s = _dot_nt(qz_b16, kc_ref[0, 0]) + _dot(tc_ref[0, g], shift) + _dot(tc_ref[1, g], shift)
        s = jnp.where(cmp_future, NEG, s)
        m = jnp.max(s, axis=1, keepdims=True)
        e = jnp.exp(s - m)
        p = jnp.where(s > 0.5 * NEG, e, 0.0) / jnp.sum(e, axis=1, keepdims=True)
        o_cmp = _dot(p.astype(BF16), vc_ref[0, 0])

        p4 = p[0:nq] + p[nq:2 * nq] + p[2 * nq:3 * nq] + p[3 * nq:4 * nq]
        p_hi, p_lo = _split2(p4)
        imp = _dot(p_hi, ovw_ref[g]) + _dot(p_lo, ovw_ref[g])
        j = lane - HEAD_DIM * (1 - g)
        own = 2 * qi + (ql >= NSA_SEL_LEN).astype(jnp.int32)
        forced = (j == 0) | (j == own) | (j == own - 1)
        imp = jnp.where(j <= own, imp + jnp.where(forced, NSA_FORCE, 0.0), NEG)
        work = jnp.where((lane // HEAD_DIM) == (1 - g), imp, BELOW_NEG)
        picked, _ = _pick_top(work, lane, NSA_SEL_TOP)
        far_blocks = (nq // NSA_SEL_LEN) * (qi - 1)
        sel_far = jnp.where(picked & (j < far_blocks), 0.0, MASK)
        sel_near = jnp.where(picked & (j >= far_blocks), 0.0, MASK)
        q_far = _with_aux(qz, jnp.concatenate([sel_far] * GROUP, axis=0), g)
        q_near = _with_aux(qz, jnp.concatenate([sel_near] * GROUP, axis=0), g)

        m_ref[...] = jnp.full_like(m_ref, NEG)
        acc_ref[...] = jnp.zeros_like(acc_ref)

        def far_step(t, carry):
            span = pl.ds(pl.multiple_of(t * FAR_TILE, FAR_TILE), FAR_TILE)
            _flash_step(_dot_nt(q_far, kaug_ref[g, span, :]), vsel_aug_ref[g, span, :], m_ref, acc_ref)
            return carry

        far_keys = jnp.maximum(qi - 1, 0) * nq
        lax.fori_loop(0, (far_keys + FAR_TILE - 1) // FAR_TILE, far_step, 0)
        span = pl.ds(near0, 2 * nq)
        _flash_step(_dot_nt(q_near, kaug_ref[g, span, :]) + tnear_ref[first, g],
                    vsel_aug_ref[g, span, :], m_ref, acc_ref)
        o_sel = _normalize(acc_ref[...])

        span = pl.ds(pl.multiple_of(qi * nq, nq), NSA_WINDOW + nq)
        s = _dot_nt(qz_b16, kwin_pad_ref[span, :]) + twin_ref[g]
        col = lax.broadcasted_iota(jnp.int32, s.shape, 1)
        s = jnp.where(col < (n_back - qi) * nq, NEG, s)
        o_win = _normalize(_dot(_exp_sub(s, _rowmax(s)), vwin_aug_ref[g, span, :]))

        def gate_col(branch):
            cols = [jnp.broadcast_to(gates[:, 3 * (GROUP * g + r) + branch:3 * (GROUP * g + r) + branch + 1],
                                     (nq, LANES)) for r in range(GROUP)]
            return jnp.concatenate(cols, axis=0)

        o_a = gate_col(0) * o_cmp + gate_col(1) * o_sel + gate_col(2) * o_win
        o_ref[0, :, g * GROUP * HEAD_DIM:(g + 1) * GROUP * HEAD_DIM] = _unstack_heads(o_a, g, nq).astype(BF16)

        qz_b = _stack_heads(qb_ref, g, nq).astype(BF16)
        span = pl.ds(near0, 2 * nq)
        s = _dot_nt(qz_b, kb_ref[0, span, :]) + tswa_ref[first, g]
        sink = sink_ref[g]
        m = jnp.maximum(_rowmax(s), sink)
        r = _dot(_exp_sub(s, m), vb_aug_ref[g, span, :])
        o_b = r / (pltpu.roll(r, HEAD_DIM, axis=1) + jnp.exp(sink - m))
        o_ref[0, :, HQ + g * GROUP * HEAD_DIM:HQ + (g + 1) * GROUP * HEAD_DIM] = (
            _unstack_heads(o_b, g, nq).astype(BF16))


def _nsa_swa(pb, pf, kvc, rel_table, sinks):
    b, s, _ = pb.shape
    nq = Q_BLOCK
    n_blk = s // nq
    nc = s // NSA_CMP_STRIDE
    n_cmp = (s - NSA_CMP_LEN) // NSA_CMP_STRIDE + 1
    n_sel = s // NSA_SEL_LEN
    assert n_sel <= HEAD_DIM and s % FAR_TILE == 0
    h8 = HEADS_PER_MIXER
    f_a = _bias_by_dist(rel_table[:, :h8], 2 * NSA_WINDOW)
    f_b = _bias_by_dist(rel_table[:, h8:2 * h8], 2 * NSA_WINDOW)

    per_blk = nq // NSA_CMP_STRIDE
    d_c = (jnp.arange(nq)[:, None] + nq - NSA_CMP_LEN + 1) - NSA_CMP_STRIDE * jnp.arange(2 * per_blk)[None, :]
    tc = jnp.where((d_c >= 0)[None], f_a[:, jnp.maximum(d_c, 0)], NEG)
    tc = jnp.pad(_rows_by_group(tc), ((0, 0), (0, 0), (0, LANES - 2 * per_blk)))
    tc_hi = tc.astype(BF16)
    tc = jnp.stack([tc_hi, (tc - tc_hi.astype(F32)).astype(BF16)])

    d_near = _dist(nq, 2 * nq, nq)
    t_a = _toeplitz(f_a, nq, 2 * nq, nq)
    t_b = _toeplitz(f_b, nq, 2 * nq, nq)
    d_first = _dist(nq, 2 * nq, 0)
    tnear = jnp.stack([_rows_by_group(_masked(_toeplitz(f_a, nq, 2 * nq, 0), d_first >= 0)),
                       _rows_by_group(_masked(t_a, d_near >= 0))])
    tswa = jnp.stack([_rows_by_group(_masked(_toeplitz(f_b, nq, 2 * nq, 0), d_first >= 0)),
                      _rows_by_group(_masked(t_b, (d_near >= 0) & (d_near < SWA_WINDOW)))])
    d_win = _dist(nq, NSA_WINDOW + nq, NSA_WINDOW)
    twin = _rows_by_group(_masked(_toeplitz(f_a, nq, NSA_WINDOW + nq, NSA_WINDOW),
                                  (d_win >= 0) & (d_win < NSA_WINDOW)))

    cmp_start = jnp.arange(nc) * NSA_CMP_STRIDE
    sel_start = jnp.arange(n_sel) * NSA_SEL_LEN
    overlap = jnp.maximum(jnp.minimum(cmp_start[:, None] + NSA_CMP_LEN, sel_start[None, :] + NSA_SEL_LEN)
                          - jnp.maximum(cmp_start[:, None], sel_start[None, :]), 0)
    overlap = jnp.where(jnp.arange(nc)[:, None] < n_cmp, overlap, 0).astype(F32) / NSA_CMP_LEN
    ovw = jnp.zeros((KV_HEADS, nc, LANES), F32)
    ovw = ovw.at[0, :, HEAD_DIM:HEAD_DIM + n_sel].set(overlap).at[1, :, :n_sel].set(overlap)
    ovw = ovw.astype(BF16)

    rows = GROUP * nq
    sink = (sinks.astype(F32) - rel_table[REL_BUCKETS - 1, h8:2 * h8]).reshape(KV_HEADS, GROUP, 1, 1)
    sink = jnp.broadcast_to(sink, (KV_HEADS, GROUP, nq, LANES)).reshape(KV_HEADS, rows, LANES)

    kv = lambda c: pl.BlockSpec((1, s, LANES), lambda i, j: (i, 0, c))
    whole = lambda a: pl.BlockSpec(a.shape, lambda i, j: (0,) * a.ndim)
    aug = pltpu.VMEM((KV_HEADS, s, LANES), BF16)
    return pl.pallas_call(
        _nsa_swa_kernel,
        grid=(b, n_blk),
        in_specs=[pl.BlockSpec((1, nq, HQ), lambda i, j: (i, j, 0)),
                  pl.BlockSpec((1, nq, HQ), lambda i, j: (i, j, 1)),
                  kv(8), kv(9), kv(10), kv(11), kv(12), kv(13),
                  pl.BlockSpec((1, nq, LANES), lambda i, j: (i, j, 2)),
                  pl.BlockSpec((1, 1, nc, LANES), lambda i, j: (i, 0, 0, 0)),
                  pl.BlockSpec((1, 1, nc, LANES), lambda i, j: (i, 1, 0, 0)),
                  whole(tc), whole(tnear), whole(twin), whole(tswa), whole(ovw), whole(sink)],
        out_specs=pl.BlockSpec((1, nq, 2 * HQ), lambda i, j: (i, j, 0)),
        out_shape=jax.ShapeDtypeStruct((b, s, 2 * HQ), BF16),
        scratch_shapes=[aug, aug,
                        pltpu.VMEM((s + NSA_WINDOW, LANES), BF16),
                        pltpu.VMEM((KV_HEADS, s + NSA_WINDOW, LANES), BF16),
                        aug,
                        pltpu.VMEM((rows, LANES), F32), pltpu.VMEM((rows, LANES), F32)],
        compiler_params=_cparams(("arbitrary", "arbitrary")),
        name="nsa_swa",
    )(pb, pb, pb, pb, pb, pb, pb, pb, pf, kvc, kvc, tc, tnear, twin, tswa, ovw, sink)


def _moba_kernel(q_ref, k_ref, v_ref, tnear_ref, o_ref, kaug_ref, vaug_ref, kmx_ref, m_ref, acc_ref):
    qi = pl.program_id(1)
    nq = MOBA_BLOCK
    rows = GROUP * nq
    nb = k_ref.shape[1] // nq

    @pl.when(qi == 0)
    def _():
        _fill_aug(k_ref, kaug_ref, MOBA_BLOCK)
        _fill_aug(v_ref, vaug_ref, 0)
        kmx_ref[...] = jnp.zeros_like(kmx_ref)
        lane = lax.broadcasted_iota(jnp.int32, (1, LANES), 1)
        for blk in range(nb):
            mean = jnp.mean(k_ref[0, blk * nq:(blk + 1) * nq, :].astype(F32), axis=0, keepdims=True)
            for g in range(KV_HEADS):
                hi, lo = _split2(jnp.where((lane // HEAD_DIM) == g, mean, 0.0))
                r = HEAD_DIM * (1 - g) + blk
                kmx_ref[0, g, r:r + 1, :] = hi
                kmx_ref[1, g, r:r + 1, :] = lo

    lane = lax.broadcasted_iota(jnp.int32, (rows, LANES), 1)
    first = jnp.minimum(qi, 1)
    near0 = pl.multiple_of(jnp.maximum(qi - 1, 0) * nq, nq)

    for g in range(KV_HEADS):
        qz = _stack_heads(q_ref, g, nq)
        qz_b16 = qz.astype(BF16)
        gs = _dot_nt(qz_b16, kmx_ref[0, g]) + _dot_nt(qz_b16, kmx_ref[1, g])
        j = lane - HEAD_DIM * (1 - g)
        work = jnp.where((j >= 0) & (j < qi), gs, BELOW_NEG)
        _, picked = _pick_top(work, lane, MOBA_TOP)
        q_far = _with_aux(qz, jnp.where(picked & (j < qi - 1), 0.0, MASK), g)
        q_near = _with_aux(qz, jnp.where((picked & (j >= qi - 1)) | (j == qi), 0.0, MASK), g)

        m_ref[...] = jnp.full_like(m_ref, NEG)
        acc_ref[...] = jnp.zeros_like(acc_ref)

        def far_step(t, carry):
            span = pl.ds(pl.multiple_of(t * FAR_TILE, FAR_TILE), FAR_TILE)
            _flash_step(_dot_nt(q_far, kaug_ref[g, span, :]), vaug_ref[g, span, :], m_ref, acc_ref)
            return carry

        far_keys = jnp.maximum(qi - 1, 0) * nq
        lax.fori_loop(0, (far_keys + FAR_TILE - 1) // FAR_TILE, far_step, 0)
        span = pl.ds(near0, 2 * nq)
        _flash_step(_dot_nt(q_near, kaug_ref[g, span, :]) + tnear_ref[first, g],
                    vaug_ref[g, span, :], m_ref, acc_ref)
        o = _normalize(acc_ref[...])
        o_ref[0, :, g * GROUP * HEAD_DIM:(g + 1) * GROUP * HEAD_DIM] = _unstack_heads(o, g, nq).astype(BF16)


def _moba(pb, rel_table, q_col, k_col, v_col):
    b, s, _ = pb.shape
    nq = MOBA_BLOCK
    nb = s // nq
    assert s % FAR_TILE == 0 and nb >= 2 and nb <= HEAD_DIM
    f = _bias_by_dist(rel_table[:, :HEADS_PER_MIXER], 2 * nq)
    d_near = _dist(nq, 2 * nq, nq)
    d_first = _dist(nq, 2 * nq, 0)
    tnear = jnp.stack([_rows_by_group(_masked(_toeplitz(f, nq, 2 * nq, 0), d_first >= 0)),
                       _rows_by_group(_masked(_toeplitz(f, nq, 2 * nq, nq), d_near >= 0))])
    rows = GROUP * nq
    whole = lambda a: pl.BlockSpec(a.shape, lambda i, j: (0,) * a.ndim)
    aug = pltpu.VMEM((KV_HEADS, s, LANES), BF16)
    return pl.pallas_call(
        _moba_kernel,
        grid=(b, nb),
        in_specs=[pl.BlockSpec((1, nq, HQ), lambda i, j: (i, j, q_col)),
                  pl.BlockSpec((1, s, LANES), lambda i, j: (i, 0, k_col)),
                  pl.BlockSpec((1, s, LANES), lambda i, j: (i, 0, v_col)),
                  whole(tnear)],
        out_specs=pl.BlockSpec((1, nq, HQ), lambda i, j: (i, j, 0)),
        out_shape=jax.ShapeDtypeStruct((b, s, HQ), BF16),
        scratch_shapes=[aug, aug,
                        pltpu.VMEM((2, KV_HEADS, LANES, LANES), BF16),
                        pltpu.VMEM((rows, LANES), F32), pltpu.VMEM((rows, LANES), F32)],
        compiler_params=_cparams(("arbitrary", "arbitrary")),
        name="moba",
    )(pb, pb, pb, tnear)


def _stick_kernel(q_ref, k_ref, v_ref, tri_ref, o_ref, acc_ref, c_ref):
    qi = pl.program_id(1)
    t_len = SB_T
    heads = HEADS_PER_MIXER
    lane = lax.broadcasted_iota(jnp.int32, (t_len, LANES), 1)
    strict = lax.broadcasted_iota(jnp.int32, (t_len, t_len), 1) < lax.broadcasted_iota(jnp.int32, (t_len, t_len), 0)

    def tile(t, diag):
        rows = pl.ds(pl.multiple_of(t * t_len, t_len), t_len)
        for pair in range(heads // 2):
            cols = slice(pair * LANES, (pair + 1) * LANES)
            q = q_ref[0, :, cols].astype(F32) * SCALE
            k = k_ref[0, rows, cols]
            v = v_ref[0, rows, cols]
            for half in range(2):
                h = 2 * pair + half
                z = _dot_nt(jnp.where((lane // HEAD_DIM) == half, q, 0.0).astype(BF16), k)
                log_1m = -(jnp.maximum(z, 0.0) + jnp.log1p(jnp.exp(-jnp.abs(z))))
                if diag:
                    log_1m = jnp.where(strict, log_1m, 0.0)
                hi, lo = _split2(log_1m)
                sums = _dot(hi, tri_ref[...]) + _dot(lo, tri_ref[...])
                if diag:
                    c = jnp.zeros((t_len, LANES), F32)
                else:
                    c = c_ref[h]
                after = sums[:, :t_len] + jnp.concatenate([c] * (t_len // LANES), axis=1)
                w = jnp.exp(log_1m + z + after)
                if diag:
                    w = jnp.where(strict, w, 0.0)
                    acc_ref[h] = _dot(w.astype(BF16), v)
                else:
                    acc_ref[h] += _dot(w.astype(BF16), v)
                c_ref[h] = c + sums[:, t_len:]

    tile(qi, True)

    def cond(carry):
        t, c_max = carry
        return (t >= 0) & (c_max >= EXP_ZERO)

    def body(carry):
        t, _ = carry
        tile(t, False)
        return t - 1, jnp.max(c_ref[...])

    lax.while_loop(cond, body, (qi - 1, jnp.float32(0.0)))
    for pair in range(heads // 2):
        o_ref[0, :, pair * LANES:(pair + 1) * LANES] = jnp.where(
            (lane // HEAD_DIM) == 0, acc_ref[2 * pair], acc_ref[2 * pair + 1]).astype(BF16)


def _stick_breaking(pb, q_col, k_col, v_col):
    b, s, _ = pb.shape
    t_len = SB_T
    assert s % t_len == 0
    tri = (jnp.arange(t_len)[:, None] > jnp.arange(t_len)[None, :]).astype(BF16)
    tri = jnp.concatenate([tri, jnp.ones((t_len, LANES), BF16)], axis=1)
    return pl.pallas_call(
        _stick_kernel,
        grid=(b, s // t_len),
        in_specs=[pl.BlockSpec((1, t_len, HQ), lambda i, j: (i, j, q_col)),
                  pl.BlockSpec((1, s, HQ), lambda i, j: (i, 0, k_col)),
                  pl.BlockSpec((1, s, HQ), lambda i, j: (i, 0, v_col)),
                  pl.BlockSpec(tri.shape, lambda i, j: (0, 0))],
        out_specs=pl.BlockSpec((1, t_len, HQ), lambda i, j: (i, j, 0)),
        out_shape=jax.ShapeDtypeStruct((b, s, HQ), BF16),
        scratch_shapes=[pltpu.VMEM((HEADS_PER_MIXER, t_len, LANES), F32),
                        pltpu.VMEM((HEADS_PER_MIXER, t_len, LANES), F32)],
        compiler_params=_cparams(("arbitrary", "arbitrary")),
        name="stick_breaking",
    )(pb, pb, pb, tri)


def _split_cols(w, sizes):
    offs = [0]
    for z in sizes:
        offs.append(offs[-1] + z)
    return [w[:, offs[i]:offs[i + 1]] for i in range(len(sizes))]


def kernel(x, c, rel_table, mod_w, mod_b, norm_w, w_in_ab, w_out_ab, nsa_cmp_wk, nsa_cmp_wv,
           nsa_cmp_pe, swa_sinks, w_in_cd, w_out_cd, ffn_w_in, ffn_w_out):
    depth = mod_w.shape[0]
    d = x.shape[-1]
    h8 = HEADS_PER_MIXER
    mod = _modulation(c, mod_w.reshape(depth * 2, d, 3 * d), mod_b.reshape(depth * 2, 3 * d))
    mod = mod.reshape(depth, 2, c.shape[0], 3, d)

    for layer in range(depth):
        shift, scale, gate = mod[layer, 0, :, 0], mod[layer, 0, :, 1], mod[layer, 0, :, 2]
        i = layer // 2
        if layer % 2 == 0:
            qa, kca, vca, ksa, vsa, kwa, vwa, ga, qb, kb, vb = _split_cols(
                w_in_ab[i], [HQ, KVW, KVW, KVW, KVW, KVW, KVW, 3 * h8, HQ, KVW, KVW])
            ga = jnp.pad(ga, ((0, 0), (0, LANES - 3 * h8)))
            w = jnp.concatenate([qa, qb, ksa, vsa, kwa, vwa, kb, vb, kca, vca, ga], axis=1).astype(BF16)
            nb = 2 * HQ + 6 * KVW
            pb, pf = _in_proj(x, norm_w[layer, 0, 0], scale, shift, w, nb, 3 * LANES, LANES)
            kvc = _compress(pf, nsa_cmp_wk[i], nsa_cmp_wv[i], nsa_cmp_pe[i])
            o = _nsa_swa(pb, pf, kvc, rel_table, swa_sinks[i])
            x = _out_proj(o, 0, o, 1, w_out_ab[i].astype(BF16), x, norm_w[layer, 0, 1], gate)
        else:
            qc, kc, vc, qd, kd, vd = _split_cols(w_in_cd[i], [HQ, KVW, KVW, HQ, HQ, HQ])
            w = jnp.concatenate([qc, qd, kd, vd, kc, vc], axis=1).astype(BF16)
            (pb,) = _in_proj(x, norm_w[layer, 0, 0], scale, shift, w, 4 * HQ + 2 * KVW, 0, 0)
            o_c = _moba(pb, rel_table, 0, 4 * HQ // LANES, 4 * HQ // LANES + 1)
            o_d = _stick_breaking(pb, 1, 2, 3)
            x = _out_proj(o_c, 0, o_d, 0, w_out_cd[i].astype(BF16), x, norm_w[layer, 0, 1], gate)

        shift, scale, gate = mod[layer, 1, :, 0], mod[layer, 1, :, 1], mod[layer, 1, :, 2]
        x = _ffn(x, norm_w[layer, 1, 0], scale, shift, ffn_w_in[layer].astype(BF16),
                 ffn_w_out[layer].astype(BF16), norm_w[layer, 1, 1], gate)
    return x
```

```python
import functools
import math

import jax
import jax.numpy as jnp
from jax import lax
from jax.experimental import pallas as pl
from jax.experimental.pallas import tpu as pltpu

F32 = jnp.float32
BF16 = jnp.bfloat16

HEAD_DIM = 64
LANES = 128
KV_HEADS = 2
GROUP = 4
HEADS_PER_MIXER = KV_HEADS * GROUP
HQ = HEADS_PER_MIXER * HEAD_DIM
KVW = KV_HEADS * HEAD_DIM
NSA_Q = 256
NSA_CMP_LEN = 32
NSA_CMP_STRIDE = 16
NSA_SEL_LEN = 64
NSA_SEL_TOP = 8
NSA_WINDOW = 512
NSA_FORCE = 1e4
SWA_WINDOW = 128
MOBA_BLOCK = 256
MOBA_TOP = 3
REL_BUCKETS = 32
REL_MAX_DIST = 128
RMS_EPS = 1e-6
NEG = -1e30
MASK = -(2.0 ** 100)
BELOW_NEG = -3e38
SCALE = HEAD_DIM ** -0.5
FAR_TILE = 512
SB_T = 256
EXP_ZERO = -104.0
VMEM_LIMIT = 60 * 1024 * 1024

_NT = (((1,), (1,)), ((), ()))


def _cparams(sem):
    return pltpu.CompilerParams(dimension_semantics=sem, vmem_limit_bytes=VMEM_LIMIT)


def _dot(a, b):
    return jnp.dot(a, b, preferred_element_type=F32)


def _dot_nt(a, b):
    return lax.dot_general(a, b, _NT, preferred_element_type=F32)


def _split2(x):
    hi = x.astype(BF16)
    lo = (x - hi.astype(F32)).astype(BF16)
    return hi, lo


def _rms(x, w):
    return x * lax.rsqrt(jnp.mean(x * x, axis=-1, keepdims=True) + RMS_EPS) * w


def _whole(a):
    return pl.BlockSpec(a.shape, lambda *_: (0,) * a.ndim)


def _mod_kernel(c_ref, w_ref, b_ref, o_ref):
    o_ref[0] = jnp.dot(c_ref[...], w_ref[0], preferred_element_type=F32,
                       precision=lax.Precision.HIGHEST) + b_ref[0]


def _modulation(c, mod_w, mod_b):
    n, d, d3 = mod_w.shape
    b = c.shape[0]
    tn = 1024
    return pl.pallas_call(
        _mod_kernel,
        grid=(n, d3 // tn),
        in_specs=[pl.BlockSpec((b, d), lambda i, j: (0, 0)),
                  pl.BlockSpec((1, d, tn), lambda i, j: (i, 0, j)),
                  pl.BlockSpec((1, 1, tn), lambda i, j: (i, 0, j))],
        out_specs=pl.BlockSpec((1, b, tn), lambda i, j: (i, 0, j)),
        out_shape=jax.ShapeDtypeStruct((n, b, d3), F32),
        compiler_params=_cparams(("arbitrary", "arbitrary")),
        name="modulation",
    )(c, mod_w, mod_b.reshape(n, 1, d3))


def _in_proj_kernel(x_ref, nw_ref, sc_ref, sh_ref, w_ref, *out_refs, nb, nf, gate_cols):
    x = x_ref[0]
    h = _rms(x, nw_ref[...]) * (1.0 + sc_ref[0]) + sh_ref[0]
    hb = h.astype(BF16)
    ob_ref = out_refs[0]
    for c0 in range(0, nb, 512):
        c1 = min(c0 + 512, nb)
        ob_ref[0, :, c0:c1] = _dot(hb, w_ref[:, c0:c1]).astype(BF16)
    if nf:
        of_ref = out_refs[1]
        y = _dot(hb, w_ref[:, nb:nb + nf])
        if gate_cols:
            y_g = jax.nn.sigmoid(y[:, nf - gate_cols:])
            of_ref[0, :, :nf - gate_cols] = y[:, :nf - gate_cols]
            of_ref[0, :, nf - gate_cols:] = y_g
        else:
            of_ref[0] = y


def _in_proj(x, norm_w, scale, shift, w, nb, nf, gate_cols, tm=512):
    b, s, d = x.shape
    tm = min(tm, s)
    out_shape = [jax.ShapeDtypeStruct((b, s, nb), BF16)]
    out_specs = [pl.BlockSpec((1, tm, nb), lambda i, j: (i, j, 0))]
    if nf:
        out_shape.append(jax.ShapeDtypeStruct((b, s, nf), F32))
        out_specs.append(pl.BlockSpec((1, tm, nf), lambda i, j: (i, j, 0)))
    return pl.pallas_call(
        functools.partial(_in_proj_kernel, nb=nb, nf=nf, gate_cols=gate_cols),
        grid=(b, s // tm),
        in_specs=[pl.BlockSpec((1, tm, d), lambda i, j: (i, j, 0)),
                  pl.BlockSpec((1, d), lambda i, j: (0, 0)),
                  pl.BlockSpec((1, 1, d), lambda i, j: (i, 0, 0)),
                  pl.BlockSpec((1, 1, d), lambda i, j: (i, 0, 0)),
                  pl.BlockSpec((d, nb + nf), lambda i, j: (0, 0))],
        out_specs=out_specs,
        out_shape=out_shape,
        compiler_params=_cparams(("arbitrary", "arbitrary")),
        name="in_proj",
    )(x, norm_w.reshape(1, d), scale.reshape(b, 1, d), shift.reshape(b, 1, d), w)


def _out_proj_kernel(o1_ref, o2_ref, w1_ref, w2_ref, x_ref, nw_ref, gate_ref, xo_ref):
    y = _dot(o1_ref[0], w1_ref[...]) + _dot(o2_ref[0], w2_ref[...])
    xo_ref[0] = x_ref[0] + gate_ref[0] * _rms(y, nw_ref[...])


def _out_proj(o1, c1, o2, c2, w, x, norm_w, gate, tm=512):
    b, s, d = x.shape
    tm = min(tm, s)
    return pl.pallas_call(
        _out_proj_kernel,
        grid=(b, s // tm),
        in_specs=[pl.BlockSpec((1, tm, HQ), lambda i, j: (i, j, c1)),
                  pl.BlockSpec((1, tm, HQ), lambda i, j: (i, j, c2)),
                  pl.BlockSpec((HQ, d), lambda i, j: (0, 0)),
                  pl.BlockSpec((HQ, d), lambda i, j: (1, 0)),
                  pl.BlockSpec((1, tm, d), lambda i, j: (i, j, 0)),
                  pl.BlockSpec((1, d), lambda i, j: (0, 0)),
                  pl.BlockSpec((1, 1, d), lambda i, j: (i, 0, 0))],
        out_specs=pl.BlockSpec((1, tm, d), lambda i, j: (i, j, 0)),
        out_shape=jax.ShapeDtypeStruct((b, s, d), F32),
        compiler_params=_cparams(("arbitrary", "arbitrary")),
        name="out_proj",
    )(o1, o2, w, w, x, norm_w.reshape(1, d), gate.reshape(b, 1, d))


def _ffn_kernel(x_ref, nw1_ref, sc_ref, sh_ref, wg_ref, wu_ref, wo_ref, nw2_ref, gate_ref,
                xo_ref, h_ref, acc_ref):
    f = pl.program_id(2)

    @pl.when(f == 0)
    def _():
        h = _rms(x_ref[0], nw1_ref[...]) * (1.0 + sc_ref[0]) + sh_ref[0]
        h_ref[...] = h.astype(BF16)
        acc_ref[...] = jnp.zeros_like(acc_ref)

    hb = h_ref[...]
    g = _dot(hb, wg_ref[...])
    u = _dot(hb, wu_ref[...])
    a = (g * jax.nn.sigmoid(g) * u).astype(BF16)
    acc_ref[...] += _dot(a, wo_ref[...])

    @pl.when(f == pl.num_programs(2) - 1)
    def _():
        xo_ref[0] = x_ref[0] + gate_ref[0] * _rms(acc_ref[...], nw2_ref[...])


def _ffn(x, nw1, scale, shift, w_in, w_out, nw2, gate, tm=512, tf=1408):
    b, s, d = x.shape
    dff = w_out.shape[0]
    tm = min(tm, s)
    nf = dff // tf
    vec = lambda a: a.reshape(b, 1, d)
    return pl.pallas_call(
        _ffn_kernel,
        grid=(b, s // tm, nf),
        in_specs=[pl.BlockSpec((1, tm, d), lambda i, j, f: (i, j, 0)),
                  pl.BlockSpec((1, d), lambda i, j, f: (0, 0)),
                  pl.BlockSpec((1, 1, d), lambda i, j, f: (i, 0, 0)),
                  pl.BlockSpec((1, 1, d), lambda i, j, f: (i, 0, 0)),
                  pl.BlockSpec((d, tf), lambda i, j, f: (0, f)),
                  pl.BlockSpec((d, tf), lambda i, j, f: (0, nf + f)),
                  pl.BlockSpec((tf, d), lambda i, j, f: (f, 0)),
                  pl.BlockSpec((1, d), lambda i, j, f: (0, 0)),
                  pl.BlockSpec((1, 1, d), lambda i, j, f: (i, 0, 0))],
        out_specs=pl.BlockSpec((1, tm, d), lambda i, j, f: (i, j, 0)),
        out_shape=jax.ShapeDtypeStruct((b, s, d), F32),
        scratch_shapes=[pltpu.VMEM((tm, d), BF16), pltpu.VMEM((tm, d), F32)],
        compiler_params=_cparams(("arbitrary", "arbitrary", "arbitrary")),
        name="ffn",
    )(x, nw1.reshape(1, d), vec(scale), vec(shift), w_in, w_in, w_out, nw2.reshape(1, d), vec(gate))


def _rel_bucket(dist):
    n = jnp.maximum(dist, 0)
    max_exact = REL_BUCKETS // 2
    nf = jnp.maximum(n, 1).astype(jnp.float32)
    large = max_exact + (jnp.log(nf / max_exact) / math.log(REL_MAX_DIST / max_exact)
                         * (REL_BUCKETS - max_exact)).astype(jnp.int32)
    large = jnp.minimum(large, REL_BUCKETS - 1)
    return jnp.where(n < max_exact, n, large)


def _bias_by_dist(tab, n):
    return (tab[_rel_bucket(jnp.arange(n))] - tab[REL_BUCKETS - 1]).T.astype(F32)


def _toeplitz(f, nq, nk, offset):
    m = nq + nk - 1
    lo = offset - (nk - 1)
    idx = jnp.clip(jnp.arange(m) + lo, 0, f.shape[1] - 1)
    w = f[:, idx]
    flat = jnp.tile(w, (1, nq + 1))[:, :nq * (m + 1)]
    hankel = flat.reshape(f.shape[0], nq, m + 1)[:, :, :nk]
    return hankel[:, :, ::-1]


def _rows_by_group(t):
    h, nq, nk = t.shape
    return t.reshape(KV_HEADS, GROUP * nq, nk)


def _dist(nq, nk, offset):
    return (jnp.arange(nq)[:, None] + offset) - jnp.arange(nk)[None, :]


def _masked(t, valid):
    return jnp.where(valid[None], t, NEG)


def _band_table(f, nq, window):
    d = _dist(nq, window + nq, window)
    return _rows_by_group(_masked(_toeplitz(f, nq, window + nq, window), (d >= 0) & (d < window)))


def _near_tables(f, nq):
    d_first = _dist(nq, 2 * nq, 0)
    d_near = _dist(nq, 2 * nq, nq)
    return jnp.stack([_rows_by_group(_masked(_toeplitz(f, nq, 2 * nq, 0), d_first >= 0)),
                      _rows_by_group(_masked(_toeplitz(f, nq, 2 * nq, nq), d_near >= 0))])


def _compress_kernel(k_ref, w_ref, pe_ref, o_ref):
    nc = o_ref.shape[2]
    half = NSA_CMP_LEN // 2
    lo = jnp.zeros((nc, LANES), F32)
    hi = jnp.zeros((nc, LANES), F32)
    for l in range(half):
        rows = k_ref[0, pl.ds(l, nc, stride=NSA_CMP_STRIDE), :]
        lo = lo + _dot((rows + pe_ref[l:l + 1, :]).astype(BF16), w_ref[0, l])
        hi = hi + _dot((rows + pe_ref[half + l:half + l + 1, :]).astype(BF16), w_ref[0, half + l])
    o_ref[0, 0] = (lo + pltpu.roll(hi, nc - 1, axis=0)).astype(BF16)


def _compress(f32_proj, wk, wv, pe):
    b, s, _ = f32_proj.shape
    nc = s // NSA_CMP_STRIDE
    eye = jnp.eye(KV_HEADS, dtype=F32)
    bd = lambda w: jnp.einsum('gh,lde->lgdhe', eye, w).reshape(NSA_CMP_LEN, LANES, LANES)
    w = jnp.stack([bd(wk), bd(wv)]).astype(BF16)
    pe2 = jnp.tile(pe, (1, KV_HEADS)).astype(F32)
    return pl.pallas_call(
        _compress_kernel,
        grid=(b, 2),
        in_specs=[pl.BlockSpec((1, s, LANES), lambda i, j: (i, 0, j)),
                  pl.BlockSpec((1, NSA_CMP_LEN, LANES, LANES), lambda i, j: (j, 0, 0, 0)),
                  pl.BlockSpec((NSA_CMP_LEN, LANES), lambda i, j: (0, 0))],
        out_specs=pl.BlockSpec((1, 1, nc, LANES), lambda i, j: (i, j, 0, 0)),
        out_shape=jax.ShapeDtypeStruct((b, 2, nc, LANES), BF16),
        compiler_params=_cparams(("arbitrary", "arbitrary")),
        name="nsa_compress",
    )(f32_proj, w, pe2)


def _rowmax(s):
    mx = s[:, :LANES]
    for c in range(LANES, s.shape[1], LANES):
        mx = jnp.maximum(mx, s[:, c:c + LANES])
    return jnp.broadcast_to(jnp.max(mx, axis=1, keepdims=True), (s.shape[0], LANES))


def _exp_sub(s, m):
    return jnp.concatenate([jnp.exp(s[:, c:c + LANES] - m).astype(BF16)
                            for c in range(0, s.shape[1], LANES)], axis=1)


def _flash_step(s, v_aug, m_ref, acc_ref):
    m_prev = m_ref[...]
    m_new = jnp.maximum(m_prev, _rowmax(s))
    acc_ref[...] = jnp.exp(m_prev - m_new) * acc_ref[...] + _dot(_exp_sub(s, m_new), v_aug)
    m_ref[...] = m_new


def _normalize(acc):
    return acc / pltpu.roll(acc, HEAD_DIM, axis=1)


def _head_slab(q_ref, h, g):
    slab = q_ref[0, :, LANES * (h // 2):LANES * (h // 2) + LANES].astype(F32) * SCALE
    if (h % 2) != g:
        slab = pltpu.roll(slab, HEAD_DIM, axis=1)
    return slab


def _stack_heads(q_ref, g, nq):
    lane = lax.broadcasted_iota(jnp.int32, (nq, LANES), 1)
    in_data = (lane // HEAD_DIM) == g
    return jnp.concatenate(
        [jnp.where(in_data, _head_slab(q_ref, GROUP * g + r, g), 0.0) for r in range(GROUP)], axis=0)


def _unstack_heads(o, g, nq):
    lane = lax.broadcasted_iota(jnp.int32, (nq, LANES), 1)
    slabs = []
    for m in range(GROUP // 2):
        even = o[(2 * m) * nq:(2 * m + 1) * nq]
        odd = o[(2 * m + 1) * nq:(2 * m + 2) * nq]
        if g == 0:
            odd = pltpu.roll(odd, HEAD_DIM, axis=1)
        else:
            even = pltpu.roll(even, HEAD_DIM, axis=1)
        slabs.append(jnp.where(lane < HEAD_DIM, even, odd))
    return jnp.concatenate(slabs, axis=1)


def _fill_aug(src_ref, dst_ref, block_len, row0=0):
    s = src_ref.shape[1]
    ch = min(512, s)
    for g in range(KV_HEADS):
        for c in range(s // ch):
            lane = lax.broadcasted_iota(jnp.int32, (ch, LANES), 1)
            in_data = (lane // HEAD_DIM) == g
            if block_len:
                key = lax.broadcasted_iota(jnp.int32, (ch, LANES), 0) + c * ch
                aux = (lane == (HEAD_DIM * (1 - g) + key // block_len)).astype(BF16)
            else:
                aux = jnp.ones((ch, LANES), BF16)
            dst_ref[g, row0 + c * ch:row0 + (c + 1) * ch, :] = jnp.where(
                in_data, src_ref[0, c * ch:(c + 1) * ch, :], aux)


def _pick_top(work, count):
    lane_f = lax.broadcasted_iota(jnp.int32, work.shape, 1).astype(F32)
    picked = jnp.zeros(work.shape, jnp.bool_)
    picked_real = jnp.zeros(work.shape, jnp.bool_)
    for _ in range(count):
        mx = jnp.max(work, axis=1, keepdims=True)
        idx = jnp.min(jnp.where(work == mx, lane_f, float(LANES)), axis=1, keepdims=True)
        pick = lane_f == idx
        picked = picked | pick
        picked_real = picked_real | (pick & (mx > 0.5 * NEG))
        work = jnp.where(pick, BELOW_NEG, work)
    return picked, picked_real


def _with_aux(qz, aux, g):
    lane = lax.broadcasted_iota(jnp.int32, qz.shape, 1)
    return jnp.where((lane // HEAD_DIM) == g, qz, aux).astype(BF16)


def _band_scores(qz, k_pad_ref, t_ref, g, qi, nq, window):
    span = pl.ds(pl.multiple_of(qi * nq, LANES), window + nq)
    s = _dot_nt(qz, k_pad_ref[span, :]) + t_ref[g]
    col = lax.broadcasted_iota(jnp.int32, s.shape, 1)
    return jnp.where(col < window - qi * nq, NEG, s), span


def _sweep(q_far_ref, q_near_ref, kaug_ref, vaug_ref, tnear_ref, m_ref, acc_ref, g, qi, nq):
    m_ref[...] = jnp.full_like(m_ref, NEG)
    acc_ref[...] = jnp.zeros_like(acc_ref)

    def far_step(t, carry):
        span = pl.ds(pl.multiple_of(t * FAR_TILE, FAR_TILE), FAR_TILE)
        _flash_step(_dot_nt(q_far_ref[g], kaug_ref[g, span, :]), vaug_ref[g, span, :], m_ref, acc_ref)
        return carry

    far_keys = jnp.maximum(qi - 1, 0) * nq
    lax.fori_loop(0, (far_keys + FAR_TILE - 1) // FAR_TILE, far_step, 0)
    span = pl.ds(pl.multiple_of(jnp.maximum(qi - 1, 0) * nq, nq), 2 * nq)
    _flash_step(_dot_nt(q_near_ref[g], kaug_ref[g, span, :]) + tnear_ref[jnp.minimum(qi, 1), g],
                vaug_ref[g, span, :], m_ref, acc_ref)
    return _normalize(acc_ref[...])


def _nsa_swa_kernel(qa_ref, qb_ref, ksel_ref, vsel_ref, kwin_ref, vwin_ref, kb_ref, vb_ref,
                    gates_ref, kc_ref, vc_ref, tc_ref, tnear_ref, twin_ref, tswa_ref,
                    ovw_ref, sink_ref, o_ref,
                    kaug_ref, vsel_aug_ref, kwin_pad_ref, vwin_aug_ref, kb_pad_ref, vb_aug_ref,
                    q_far_ref, q_near_ref, part_ref, m_ref, acc_ref):
    qi = pl.program_id(1)
    nq = NSA_Q
    rows = GROUP * nq
    nc = kc_ref.shape[2]

    @pl.when(qi == 0)
    def _():
        _fill_aug(ksel_ref, kaug_ref, NSA_SEL_LEN)
        _fill_aug(vsel_ref, vsel_aug_ref, 0)
        kwin_pad_ref[:NSA_WINDOW, :] = jnp.zeros((NSA_WINDOW, LANES), BF16)
        kwin_pad_ref[NSA_WINDOW:, :] = kwin_ref[0]
        vwin_aug_ref[:, :NSA_WINDOW, :] = jnp.zeros((KV_HEADS, NSA_WINDOW, LANES), BF16)
        _fill_aug(vwin_ref, vwin_aug_ref, 0, row0=NSA_WINDOW)
        kb_pad_ref[:SWA_WINDOW, :] = jnp.zeros((SWA_WINDOW, LANES), BF16)
        kb_pad_ref[SWA_WINDOW:, :] = kb_ref[0]
        vb_aug_ref[:, :SWA_WINDOW, :] = jnp.zeros((KV_HEADS, SWA_WINDOW, LANES), BF16)
        _fill_aug(vb_ref, vb_aug_ref, 0, row0=SWA_WINDOW)

    lane = lax.broadcasted_iota(jnp.int32, (nq, LANES), 1)
    ql = lax.broadcasted_iota(jnp.int32, (nq, LANES), 0)
    gates = gates_ref[0]

    def gate_col(g, branch):
        cols = [jnp.broadcast_to(gates[:, 3 * (GROUP * g + r) + branch:3 * (GROUP * g + r) + branch + 1],
                                 (nq, LANES)) for r in range(GROUP)]
        return jnp.concatenate(cols, axis=0)

    per_blk = nq // NSA_CMP_STRIDE
    u_id = lax.broadcasted_iota(jnp.int32, (LANES, nc), 0)
    n_id = lax.broadcasted_iota(jnp.int32, (LANES, nc), 1)
    shift = ((u_id < 2 * per_blk) & (n_id == per_blk * (qi - 1) + u_id)).astype(BF16)
    cmp_future = lax.broadcasted_iota(jnp.int32, (rows, nc), 1) >= per_blk * (qi + 1)

    works, qzs = [], []
    for g in range(KV_HEADS):
        qz = _stack_heads(qa_ref, g, nq)
        qz_b16 = qz.astype(BF16)
        qzs.append(qz)

        s = _dot_nt(qz_b16, kc_ref[0, 0]) + _dot(tc_ref[0, g], shift) + _dot(tc_ref[1, g], shift)
        s = jnp.where(cmp_future, NEG, s)
        m = jnp.max(s, axis=1, keepdims=True)
        e = jnp.exp(s - m)
        p = jnp.where(s > 0.5 * NEG, e, 0.0) / jnp.sum(e, axis=1, keepdims=True)
        o_cmp = _dot(p.astype(BF16), vc_ref[0, 0])

        p4 = p[0:nq] + p[nq:2 * nq] + p[2 * nq:3 * nq] + p[3 * nq:4 * nq]
        p_hi, p_lo = _split2(p4)
        imp = _dot(p_hi, ovw_ref[g]) + _dot(p_lo, ovw_ref[g])
        j = lane - HEAD_DIM * (1 - g)
        own = (nq // NSA_SEL_LEN) * qi + ql // NSA_SEL_LEN
        forced = (j == 0) | (j == own) | (j == own - 1)
        imp = jnp.where(j <= own, imp + jnp.where(forced, NSA_FORCE, 0.0), NEG)
        works.append(jnp.where((lane // HEAD_DIM) == (1 - g), imp, BELOW_NEG))

        s, span = _band_scores(qz_b16, kwin_pad_ref, twin_ref, g, qi, nq, NSA_WINDOW)
        o_win = _normalize(_dot(_exp_sub(s, _rowmax(s)), vwin_aug_ref[g, span, :]))
        part_ref[g] = gate_col(g, 0) * o_cmp + gate_col(g, 2) * o_win

        qz_b = _stack_heads(qb_ref, g, nq).astype(BF16)
        s, span = _band_scores(qz_b, kb_pad_ref, tswa_ref, g, qi, nq, SWA_WINDOW)
        sink = sink_ref[g]
        m = jnp.maximum(_rowmax(s), sink)
        r = _dot(_exp_sub(s, m), vb_aug_ref[g, span, :])
        o_b = r / (pltpu.roll(r, HEAD_DIM, axis=1) + jnp.exp(sink - m))
        o_ref[0, :, HQ + g * GROUP * HEAD_DIM:HQ + (g + 1) * GROUP * HEAD_DIM] = (
            _unstack_heads(o_b, g, nq).astype(BF16))

    picked, _ = _pick_top(jnp.concatenate(works, axis=0), NSA_SEL_TOP)
    far_blocks = (nq // NSA_SEL_LEN) * (qi - 1)
    for g in range(KV_HEADS):
        pk = picked[g * nq:(g + 1) * nq]
        j = lane - HEAD_DIM * (1 - g)
        sel_far = jnp.where(pk & (j < far_blocks), 0.0, MASK)
        sel_near = jnp.where(pk & (j >= far_blocks), 0.0, MASK)
        q_far_ref[g] = _with_aux(qzs[g], jnp.concatenate([sel_far] * GROUP, axis=0), g)
        q_near_ref[g] = _with_aux(qzs[g], jnp.concatenate([sel_near] * GROUP, axis=0), g)

    for g in range(KV_HEADS):
        o_sel = _sweep(q_far_ref, q_near_ref, kaug_ref, vsel_aug_ref, tnear_ref, m_ref, acc_ref, g, qi, nq)
        o_a = part_ref[g] + gate_col(g, 1) * o_sel
        o_ref[0, :, g * GROUP * HEAD_DIM:(g + 1) * GROUP * HEAD_DIM] = _unstack_heads(o_a, g, nq).astype(BF16)


def _nsa_swa(pb, pf, kvc, rel_table, sinks):
    b, s, _ = pb.shape
    nq = NSA_Q
    nc = s // NSA_CMP_STRIDE
    n_cmp = (s - NSA_CMP_LEN) // NSA_CMP_STRIDE + 1
    n_sel = s // NSA_SEL_LEN
    per_blk = nq // NSA_CMP_STRIDE
    assert n_sel <= HEAD_DIM and s % FAR_TILE == 0 and 2 * per_blk <= LANES
    h8 = HEADS_PER_MIXER
    f_a = _bias_by_dist(rel_table[:, :h8], NSA_WINDOW + nq)
    f_b = _bias_by_dist(rel_table[:, h8:2 * h8], NSA_WINDOW + nq)

    d_c = (jnp.arange(nq)[:, None] + nq - NSA_CMP_LEN + 1) - NSA_CMP_STRIDE * jnp.arange(2 * per_blk)[None, :]
    tc = jnp.where((d_c >= 0)[None], f_a[:, jnp.maximum(d_c, 0)], NEG)
    tc = jnp.pad(_rows_by_group(tc), ((0, 0), (0, 0), (0, LANES - 2 * per_blk)))
    tc_hi = tc.astype(BF16)
    tc = jnp.stack([tc_hi, (tc - tc_hi.astype(F32)).astype(BF16)])

    tnear = _near_tables(f_a, nq)
    twin = _band_table(f_a, nq, NSA_WINDOW)
    tswa = _band_table(f_b, nq, SWA_WINDOW)

    cmp_start = jnp.arange(nc) * NSA_CMP_STRIDE
    sel_start = jnp.arange(n_sel) * NSA_SEL_LEN
    overlap = jnp.maximum(jnp.minimum(cmp_start[:, None] + NSA_CMP_LEN, sel_start[None, :] + NSA_SEL_LEN)
                          - jnp.maximum(cmp_start[:, None], sel_start[None, :]), 0)
    overlap = jnp.where(jnp.arange(nc)[:, None] < n_cmp, overlap, 0).astype(F32) / NSA_CMP_LEN
    ovw = jnp.zeros((KV_HEADS, nc, LANES), F32)
    ovw = ovw.at[0, :, HEAD_DIM:HEAD_DIM + n_sel].set(overlap).at[1, :, :n_sel].set(overlap)
    ovw = ovw.astype(BF16)

    rows = GROUP * nq
    sink = (sinks.astype(F32) - rel_table[REL_BUCKETS - 1, h8:2 * h8]).reshape(KV_HEADS, GROUP, 1, 1)
    sink = jnp.broadcast_to(sink, (KV_HEADS, GROUP, nq, LANES)).reshape(KV_HEADS, rows, LANES)

    kv = lambda c: pl.BlockSpec((1, s, LANES), lambda i, j: (i, 0, c), pipeline_mode=pl.Buffered(1))
    aug = pltpu.VMEM((KV_HEADS, s, LANES), BF16)
    qsel = pltpu.VMEM((KV_HEADS, rows, LANES), BF16)
    return pl.pallas_call(
        _nsa_swa_kernel,
        grid=(b, s // nq),
        in_specs=[pl.BlockSpec((1, nq, HQ), lambda i, j: (i, j, 0)),
                  pl.BlockSpec((1, nq, HQ), lambda i, j: (i, j, 1)),
                  kv(8), kv(9), kv(10), kv(11), kv(12), kv(13),
                  pl.BlockSpec((1, nq, LANES), lambda i, j: (i, j, 2)),
                  pl.BlockSpec((1, 1, nc, LANES), lambda i, j: (i, 0, 0, 0)),
                  pl.BlockSpec((1, 1, nc, LANES), lambda i, j: (i, 1, 0, 0)),
                  _whole(tc), _whole(tnear), _whole(twin), _whole(tswa), _whole(ovw), _whole(sink)],
        out_specs=pl.BlockSpec((1, nq, 2 * HQ), lambda i, j: (i, j, 0)),
        out_shape=jax.ShapeDtypeStruct((b, s, 2 * HQ), BF16),
        scratch_shapes=[aug, aug,
                        pltpu.VMEM((s + NSA_WINDOW, LANES), BF16),
                        pltpu.VMEM((KV_HEADS, s + NSA_WINDOW, LANES), BF16),
                        pltpu.VMEM((s + SWA_WINDOW, LANES), BF16),
                        pltpu.VMEM((KV_HEADS, s + SWA_WINDOW, LANES), BF16),
                        qsel, qsel,
                        pltpu.VMEM((KV_HEADS, rows, LANES), F32),
                        pltpu.VMEM((rows, LANES), F32), pltpu.VMEM((rows, LANES), F32)],
        compiler_params=_cparams(("arbitrary", "arbitrary")),
        name="nsa_swa",
    )(pb, pb, pb, pb, pb, pb, pb, pb, pf, kvc, kvc, tc, tnear, twin, tswa, ovw, sink)


def _moba_kernel(q_ref, k_ref, v_ref, tnear_ref, o_ref, kaug_ref, vaug_ref, kmx_ref,
                 q_far_ref, q_near_ref, m_ref, acc_ref):
    qi = pl.program_id(1)
    nq = MOBA_BLOCK
    rows = GROUP * nq
    nb = k_ref.shape[1] // nq

    @pl.when(qi == 0)
    def _():
        _fill_aug(k_ref, kaug_ref, MOBA_BLOCK)
        _fill_aug(v_ref, vaug_ref, 0)
        kmx_ref[...] = jnp.zeros_like(kmx_ref)
        lane = lax.broadcasted_iota(jnp.int32, (1, LANES), 1)
        for blk in range(nb):
            mean = jnp.mean(k_ref[0, blk * nq:(blk + 1) * nq, :].astype(F32), axis=0, keepdims=True)
            for g in range(KV_HEADS):
                hi, lo = _split2(jnp.where((lane // HEAD_DIM) == g, mean, 0.0))
                r = HEAD_DIM * (1 - g) + blk
                kmx_ref[0, g, r:r + 1, :] = hi
                kmx_ref[1, g, r:r + 1, :] = lo

    lane = lax.broadcasted_iota(jnp.int32, (rows, LANES), 1)
    works, qzs = [], []
    for g in range(KV_HEADS):
        qz = _stack_heads(q_ref, g, nq)
        qz_b16 = qz.astype(BF16)
        gs = _dot_nt(qz_b16, kmx_ref[0, g]) + _dot_nt(qz_b16, kmx_ref[1, g])
        j = lane - HEAD_DIM * (1 - g)
        works.append(jnp.where((j >= 0) & (j < qi), gs, BELOW_NEG))
        qzs.append(qz)
    _, picked = _pick_top(jnp.concatenate(works, axis=0), MOBA_TOP)
    for g in range(KV_HEADS):
        pk = picked[g * rows:(g + 1) * rows]
        j = lane - HEAD_DIM * (1 - g)
        q_far_ref[g] = _with_aux(qzs[g], jnp.where(pk & (j < qi - 1), 0.0, MASK), g)
        q_near_ref[g] = _with_aux(qzs[g], jnp.where((pk & (j >= qi - 1)) | (j == qi), 0.0, MASK), g)

    for g in range(KV_HEADS):
        o = _sweep(q_far_ref, q_near_ref, kaug_ref, vaug_ref, tnear_ref, m_ref, acc_ref, g, qi, nq)
        o_ref[0, :, g * GROUP * HEAD_DIM:(g + 1) * GROUP * HEAD_DIM] = _unstack_heads(o, g, nq).astype(BF16)


def _moba(pb, rel_table, q_col, k_col, v_col):
    b, s, _ = pb.shape
    nq = MOBA_BLOCK
    nb = s // nq
    assert s % FAR_TILE == 0 and nb >= 2 and nb <= HEAD_DIM
    tnear = _near_tables(_bias_by_dist(rel_table[:, :HEADS_PER_MIXER], 2 * nq), nq)
    rows = GROUP * nq
    aug = pltpu.VMEM((KV_HEADS, s, LANES), BF16)
    qsel = pltpu.VMEM((KV_HEADS, rows, LANES), BF16)
    kv = lambda c: pl.BlockSpec((1, s, LANES), lambda i, j: (i, 0, c), pipeline_mode=pl.Buffered(1))
    return pl.pallas_call(
        _moba_kernel,
        grid=(b, nb),
        in_specs=[pl.BlockSpec((1, nq, HQ), lambda i, j: (i, j, q_col)), kv(k_col), kv(v_col), _whole(tnear)],
        out_specs=pl.BlockSpec((1, nq, HQ), lambda i, j: (i, j, 0)),
        out_shape=jax.ShapeDtypeStruct((b, s, HQ), BF16),
        scratch_shapes=[aug, aug,
                        pltpu.VMEM((2, KV_HEADS, LANES, LANES), BF16),
                        qsel, qsel,
                        pltpu.VMEM((rows, LANES), F32), pltpu.VMEM((rows, LANES), F32)],
        compiler_params=_cparams(("arbitrary", "arbitrary")),
        name="moba",
    )(pb, pb, pb, tnear)


def _stick_kernel(q_ref, k_ref, v_ref, tri_ref, o_ref, acc_ref, c_ref):
    qi = pl.program_id(1)
    t_len = SB_T
    pairs = HEADS_PER_MIXER // 2
    lane = lax.broadcasted_iota(jnp.int32, (t_len, LANES), 1)
    strict = lax.broadcasted_iota(jnp.int32, (t_len, t_len), 1) < lax.broadcasted_iota(jnp.int32, (t_len, t_len), 0)

    def pair_tile(t, pair, diag):
        rows = pl.ds(pl.multiple_of(t * t_len, t_len), t_len)
        cols = slice(pair * LANES, (pair + 1) * LANES)
        q = q_ref[0, :, cols].astype(F32) * SCALE
        k = k_ref[0, rows, cols]
        v = v_ref[0, rows, cols]
        for half in range(2):
            h = 2 * pair + half
            z = _dot_nt(jnp.where((lane // HEAD_DIM) == half, q, 0.0).astype(BF16), k)
            log_1m = -(jnp.maximum(z, 0.0) + jnp.log1p(jnp.exp(-jnp.abs(z))))
            if diag:
                log_1m = jnp.where(strict, log_1m, 0.0)
            hi, lo = _split2(log_1m)
            sums = _dot(hi, tri_ref[...]) + _dot(lo, tri_ref[...])
            c = jnp.zeros((t_len, LANES), F32) if diag else c_ref[h]
            after = sums[:, :t_len] + jnp.concatenate([c] * (t_len // LANES), axis=1)
            w = jnp.exp(log_1m + z + after)
            if diag:
                acc_ref[h] = _dot(jnp.where(strict, w, 0.0).astype(BF16), v)
            else:
                acc_ref[h] += _dot(w.astype(BF16), v)
            c_ref[h] = c + sums[:, t_len:]

    def c_max(pair):
        return jnp.max(c_ref[2 * pair:2 * pair + 2])

    for pair in range(pairs):
        pair_tile(qi, pair, True)

    def cond(carry):
        t, c_maxes = carry
        live = c_maxes[0] >= EXP_ZERO
        for pair in range(1, pairs):
            live = live | (c_maxes[pair] >= EXP_ZERO)
        return (t >= 0) & live

    def body(carry):
        t, c_maxes = carry
        for pair in range(pairs):
            pl.when(c_maxes[pair] >= EXP_ZERO)(functools.partial(pair_tile, t, pair, False))
        return t - 1, tuple(c_max(pair) for pair in range(pairs))

    lax.while_loop(cond, body, (qi - 1, tuple(c_max(pair) for pair in range(pairs))))
    for pair in range(pairs):
        o_ref[0, :, pair * LANES:(pair + 1) * LANES] = jnp.where(
            (lane // HEAD_DIM) == 0, acc_ref[2 * pair], acc_ref[2 * pair + 1]).astype(BF16)


def _stick_breaking(pb, q_col, k_col, v_col):
    b, s, _ = pb.shape
    t_len = SB_T
    assert s % t_len == 0
    tri = (jnp.arange(t_len)[:, None] > jnp.arange(t_len)[None, :]).astype(BF16)
    tri = jnp.concatenate([tri, jnp.ones((t_len, LANES), BF16)], axis=1)
    return pl.pallas_call(
        _stick_kernel,
        grid=(b, s // t_len),
        in_specs=[pl.BlockSpec((1, t_len, HQ), lambda i, j: (i, j, q_col)),
                  pl.BlockSpec((1, s, HQ), lambda i, j: (i, 0, k_col)),
                  pl.BlockSpec((1, s, HQ), lambda i, j: (i, 0, v_col)),
                  _whole(tri)],
        out_specs=pl.BlockSpec((1, t_len, HQ), lambda i, j: (i, j, 0)),
        out_shape=jax.ShapeDtypeStruct((b, s, HQ), BF16),
        scratch_shapes=[pltpu.VMEM((HEADS_PER_MIXER, t_len, LANES), F32),
                        pltpu.VMEM((HEADS_PER_MIXER, t_len, LANES), F32)],
        compiler_params=_cparams(("arbitrary", "arbitrary")),
        name="stick_breaking",
    )(pb, pb, pb, tri)


def _split_cols(w, sizes):
    offs = [0]
    for z in sizes:
        offs.append(offs[-1] + z)
    return [w[:, offs[i]:offs[i + 1]] for i in range(len(sizes))]


def kernel(x, c, rel_table, mod_w, mod_b, norm_w, w_in_ab, w_out_ab, nsa_cmp_wk, nsa_cmp_wv,
           nsa_cmp_pe, swa_sinks, w_in_cd, w_out_cd, ffn_w_in, ffn_w_out):
    depth = mod_w.shape[0]
    d = x.shape[-1]
    h8 = HEADS_PER_MIXER
    mod = _modulation(c, mod_w.reshape(depth * 2, d, 3 * d), mod_b.reshape(depth * 2, 3 * d))
    mod = mod.reshape(depth, 2, c.shape[0], 3, d)

    for layer in range(depth):
        shift, scale, gate = mod[layer, 0, :, 0], mod[layer, 0, :, 1], mod[layer, 0, :, 2]
        i = layer // 2
        if layer % 2 == 0:
            qa, kca, vca, ksa, vsa, kwa, vwa, ga, qb, kb, vb = _split_cols(
                w_in_ab[i], [HQ, KVW, KVW, KVW, KVW, KVW, KVW, 3 * h8, HQ, KVW, KVW])
            ga = jnp.pad(ga, ((0, 0), (0, LANES - 3 * h8)))
            w = jnp.concatenate([qa, qb, ksa, vsa, kwa, vwa, kb, vb, kca, vca, ga], axis=1).astype(BF16)
            nb = 2 * HQ + 6 * KVW
            pb, pf = _in_proj(x, norm_w[layer, 0, 0], scale, shift, w, nb, 3 * LANES, LANES)
            kvc = _compress(pf, nsa_cmp_wk[i], nsa_cmp_wv[i], nsa_cmp_pe[i])
            o = _nsa_swa(pb, pf, kvc, rel_table, swa_sinks[i])
            x = _out_proj(o, 0, o, 1, w_out_ab[i].astype(BF16), x, norm_w[layer, 0, 1], gate)
        else:
            qc, kc, vc, qd, kd, vd = _split_cols(w_in_cd[i], [HQ, KVW, KVW, HQ, HQ, HQ])
            w = jnp.concatenate([qc, qd, kd, vd, kc, vc], axis=1).astype(BF16)
            (pb,) = _in_proj(x, norm_w[layer, 0, 0], scale, shift, w, 4 * HQ + 2 * KVW, 0, 0)
            o_c = _moba(pb, rel_table, 0, 4 * HQ // LANES, 4 * HQ // LANES + 1)
            o_d = _stick_breaking(pb, 1, 2, 3)
            x = _out_proj(o_c, 0, o_d, 0, w_out_cd[i].astype(BF16), x, norm_w[layer, 0, 1], gate)

        shift, scale, gate = mod[layer, 1, :, 0], mod[layer, 1, :, 1], mod[layer, 1, :, 2]
        x = _ffn(x, norm_w[layer, 1, 0], scale, shift, ffn_w_in[layer].astype(BF16),
                 ffn_w_out[layer].astype(BF16), norm_w[layer, 1, 1], gate)
    return x
```

```python
import functools
import math

import jax
import jax.numpy as jnp
from jax import lax
from jax.experimental import pallas as pl
from jax.experimental.pallas import tpu as pltpu

F32 = jnp.float32
BF16 = jnp.bfloat16

HEAD_DIM = 64
LANES = 128
KV_HEADS = 2
GROUP = 4
HEADS_PER_MIXER = KV_HEADS * GROUP
HQ = HEADS_PER_MIXER * HEAD_DIM
KVW = KV_HEADS * HEAD_DIM
NSA_Q = 256
NSA_CMP_LEN = 32
NSA_CMP_STRIDE = 16
NSA_SEL_LEN = 64
NSA_SEL_TOP = 8
NSA_WINDOW = 512
NSA_FORCE = 1e4
SWA_WINDOW = 128
MOBA_BLOCK = 256
MOBA_TOP = 3
REL_BUCKETS = 32
REL_MAX_DIST = 128
RMS_EPS = 1e-6
NEG = -1e30
MASK = -(2.0 ** 100)
BELOW_NEG = -3e38
SCALE = HEAD_DIM ** -0.5
FAR_TILE = 512
SB_T = 256
EXP_ZERO = -104.0
VMEM_LIMIT = 60 * 1024 * 1024

_NT = (((1,), (1,)), ((), ()))


def _cparams(sem):
    return pltpu.CompilerParams(dimension_semantics=sem, vmem_limit_bytes=VMEM_LIMIT)


def _dot(a, b):
    return jnp.dot(a, b, preferred_element_type=F32)


def _dot_nt(a, b):
    return lax.dot_general(a, b, _NT, preferred_element_type=F32)


def _split2(x):
    hi = x.astype(BF16)
    lo = (x - hi.astype(F32)).astype(BF16)
    return hi, lo


def _rms(x, w):
    return x * lax.rsqrt(jnp.mean(x * x, axis=-1, keepdims=True) + RMS_EPS) * w


def _whole(a):
    return pl.BlockSpec(a.shape, lambda *_: (0,) * a.ndim)


def _mod_kernel(c_ref, w_ref, b_ref, o_ref):
    o_ref[0] = jnp.dot(c_ref[...], w_ref[0], preferred_element_type=F32,
                       precision=lax.Precision.HIGHEST) + b_ref[0]


def _modulation(c, mod_w, mod_b):
    n, d, d3 = mod_w.shape
    b = c.shape[0]
    tn = 1024
    return pl.pallas_call(
        _mod_kernel,
        grid=(n, d3 // tn),
        in_specs=[pl.BlockSpec((b, d), lambda i, j: (0, 0)),
                  pl.BlockSpec((1, d, tn), lambda i, j: (i, 0, j)),
                  pl.BlockSpec((1, 1, tn), lambda i, j: (i, 0, j))],
        out_specs=pl.BlockSpec((1, b, tn), lambda i, j: (i, 0, j)),
        out_shape=jax.ShapeDtypeStruct((n, b, d3), F32),
        compiler_params=_cparams(("arbitrary", "arbitrary")),
        name="modulation",
    )(c, mod_w, mod_b.reshape(n, 1, d3))


def _in_proj_kernel(x_ref, nw_ref, sc_ref, sh_ref, w_ref, *out_refs, nb, nf, gate_cols):
    x = x_ref[0]
    h = _rms(x, nw_ref[...]) * (1.0 + sc_ref[0]) + sh_ref[0]
    hb = h.astype(BF16)
    ob_ref = out_refs[0]
    for c0 in range(0, nb, 512):
        c1 = min(c0 + 512, nb)
        ob_ref[0, :, c0:c1] = _dot(hb, w_ref[:, c0:c1]).astype(BF16)
    if nf:
        of_ref = out_refs[1]
        y = _dot(hb, w_ref[:, nb:nb + nf])
        if gate_cols:
            y_g = jax.nn.sigmoid(y[:, nf - gate_cols:])
            of_ref[0, :, :nf - gate_cols] = y[:, :nf - gate_cols]
            of_ref[0, :, nf - gate_cols:] = y_g
        else:
            of_ref[0] = y


def _in_proj(x, norm_w, scale, shift, w, nb, nf, gate_cols, tm=512):
    b, s, d = x.shape
    tm = min(tm, s)
    out_shape = [jax.ShapeDtypeStruct((b, s, nb), BF16)]
    out_specs = [pl.BlockSpec((1, tm, nb), lambda i, j: (i, j, 0))]
    if nf:
        out_shape.append(jax.ShapeDtypeStruct((b, s, nf), F32))
        out_specs.append(pl.BlockSpec((1, tm, nf), lambda i, j: (i, j, 0)))
    return pl.pallas_call(
        functools.partial(_in_proj_kernel, nb=nb, nf=nf, gate_cols=gate_cols),
        grid=(b, s // tm),
        in_specs=[pl.BlockSpec((1, tm, d), lambda i, j: (i, j, 0)),
                  pl.BlockSpec((1, d), lambda i, j: (0, 0)),
                  pl.BlockSpec((1, 1, d), lambda i, j: (i, 0, 0)),
                  pl.BlockSpec((1, 1, d), lambda i, j: (i, 0, 0)),
                  pl.BlockSpec((d, nb + nf), lambda i, j: (0, 0))],
        out_specs=out_specs,
        out_shape=out_shape,
        compiler_params=_cparams(("arbitrary", "arbitrary")),
        name="in_proj",
    )(x, norm_w.reshape(1, d), scale.reshape(b, 1, d), shift.reshape(b, 1, d), w)


def _out_proj_kernel(o1_ref, o2_ref, w1_ref, w2_ref, x_ref, nw_ref, gate_ref, xo_ref):
    y = _dot(o1_ref[0], w1_ref[...]) + _dot(o2_ref[0], w2_ref[...])
    xo_ref[0] = x_ref[0] + gate_ref[0] * _rms(y, nw_ref[...])


def _out_proj(o1, c1, o2, c2, w, x, norm_w, gate, tm=512):
    b, s, d = x.shape
    tm = min(tm, s)
    return pl.pallas_call(
        _out_proj_kernel,
        grid=(b, s // tm),
        in_specs=[pl.BlockSpec((1, tm, HQ), lambda i, j: (i, j, c1)),
                  pl.BlockSpec((1, tm, HQ), lambda i, j: (i, j, c2)),
                  pl.BlockSpec((HQ, d), lambda i, j: (0, 0)),
                  pl.BlockSpec((HQ, d), lambda i, j: (1, 0)),
                  pl.BlockSpec((1, tm, d), lambda i, j: (i, j, 0)),
                  pl.BlockSpec((1, d), lambda i, j: (0, 0)),
                  pl.BlockSpec((1, 1, d), lambda i, j: (i, 0, 0))],
        out_specs=pl.BlockSpec((1, tm, d), lambda i, j: (i, j, 0)),
        out_shape=jax.ShapeDtypeStruct((b, s, d), F32),
        compiler_params=_cparams(("arbitrary", "arbitrary")),
        name="out_proj",
    )(o1, o2, w, w, x, norm_w.reshape(1, d), gate.reshape(b, 1, d))


def _ffn_kernel(x_ref, nw1_ref, sc_ref, sh_ref, wi_ref, wo_ref, nw2_ref, gate_ref, xo_ref, *, tf):
    dff = wo_ref.shape[0]
    hb = (_rms(x_ref[0], nw1_ref[...]) * (1.0 + sc_ref[0]) + sh_ref[0]).astype(BF16)
    y = None
    for c in range(0, dff, tf):
        g = _dot(hb, wi_ref[:, c:c + tf])
        u = _dot(hb, wi_ref[:, dff + c:dff + c + tf])
        a = (g * jax.nn.sigmoid(g) * u).astype(BF16)
        part = _dot(a, wo_ref[c:c + tf, :])
        y = part if y is None else y + part
    xo_ref[0] = x_ref[0] + gate_ref[0] * _rms(y, nw2_ref[...])


def _ffn(x, nw1, scale, shift, w_in, w_out, nw2, gate, tm=512, tf=1408):
    b, s, d = x.shape
    dff = w_out.shape[0]
    tm = min(tm, s)
    assert dff % tf == 0 and tf % LANES == 0
    vec = lambda a: a.reshape(b, 1, d)
    return pl.pallas_call(
        functools.partial(_ffn_kernel, tf=tf),
        grid=(b, s // tm),
        in_specs=[pl.BlockSpec((1, tm, d), lambda i, j: (i, j, 0)),
                  pl.BlockSpec((1, d), lambda i, j: (0, 0)),
                  pl.BlockSpec((1, 1, d), lambda i, j: (i, 0, 0)),
                  pl.BlockSpec((1, 1, d), lambda i, j: (i, 0, 0)),
                  _whole(w_in), _whole(w_out),
                  pl.BlockSpec((1, d), lambda i, j: (0, 0)),
                  pl.BlockSpec((1, 1, d), lambda i, j: (i, 0, 0))],
        out_specs=pl.BlockSpec((1, tm, d), lambda i, j: (i, j, 0)),
        out_shape=jax.ShapeDtypeStruct((b, s, d), F32),
        compiler_params=_cparams(("arbitrary", "arbitrary")),
        name="ffn",
    )(x, nw1.reshape(1, d), vec(scale), vec(shift), w_in, w_out, nw2.reshape(1, d), vec(gate))


def _rel_bucket(dist):
    n = jnp.maximum(dist, 0)
    max_exact = REL_BUCKETS // 2
    nf = jnp.maximum(n, 1).astype(jnp.float32)
    large = max_exact + (jnp.log(nf / max_exact) / math.log(REL_MAX_DIST / max_exact)
                         * (REL_BUCKETS - max_exact)).astype(jnp.int32)
    large = jnp.minimum(large, REL_BUCKETS - 1)
    return jnp.where(n < max_exact, n, large)


def _bias_by_dist(tab, n):
    return (tab[_rel_bucket(jnp.arange(n))] - tab[REL_BUCKETS - 1]).T.astype(F32)


def _toeplitz(f, nq, nk, offset):
    p = nq + nk - 1
    k = jnp.arange(p)
    delta = jnp.where(k < nk, k, k - p)
    w = f[:, jnp.clip(offset - delta, 0, f.shape[1] - 1)]
    flat = jnp.tile(w, (1, nq))[:, :nq * (p - 1)]
    return flat.reshape(f.shape[0], nq, p - 1)[:, :, :nk]


def _rows_by_group(t):
    h, nq, nk = t.shape
    return t.reshape(KV_HEADS, GROUP * nq, nk)


def _dist(nq, nk, offset):
    return (jnp.arange(nq)[:, None] + offset) - jnp.arange(nk)[None, :]


def _masked(t, valid):
    return jnp.where(valid[None], t, NEG)


def _band_table(f, nq, window):
    d = _dist(nq, window + nq, window)
    return _rows_by_group(_masked(_toeplitz(f, nq, window + nq, window), (d >= 0) & (d < window)))


def _near_tables(f, nq):
    d_first = _dist(nq, 2 * nq, 0)
    d_near = _dist(nq, 2 * nq, nq)
    return jnp.stack([_rows_by_group(_masked(_toeplitz(f, nq, 2 * nq, 0), d_first >= 0)),
                      _rows_by_group(_masked(_toeplitz(f, nq, 2 * nq, nq), d_near >= 0))])


def _compress_kernel(k_ref, w_ref, pe_ref, o_ref):
    nc = o_ref.shape[2]
    half = NSA_CMP_LEN // 2
    lo = jnp.zeros((nc, LANES), F32)
    hi = jnp.zeros((nc, LANES), F32)
    for l in range(half):
        rows = k_ref[0, pl.ds(l, nc, stride=NSA_CMP_STRIDE), :]
        lo = lo + _dot((rows + pe_ref[l:l + 1, :]).astype(BF16), w_ref[0, l])
        hi = hi + _dot((rows + pe_ref[half + l:half + l + 1, :]).astype(BF16), w_ref[0, half + l])
    o_ref[0, 0] = (lo + pltpu.roll(hi, nc - 1, axis=0)).astype(BF16)


def _compress(f32_proj, wk, wv, pe):
    b, s, _ = f32_proj.shape
    nc = s // NSA_CMP_STRIDE
    eye = jnp.eye(KV_HEADS, dtype=F32)
    bd = lambda w: jnp.einsum('gh,lde->lgdhe', eye, w).reshape(NSA_CMP_LEN, LANES, LANES)
    w = jnp.stack([bd(wk), bd(wv)]).astype(BF16)
    pe2 = jnp.tile(pe, (1, KV_HEADS)).astype(F32)
    return pl.pallas_call(
        _compress_kernel,
        grid=(b, 2),
        in_specs=[pl.BlockSpec((1, s, LANES), lambda i, j: (i, 0, j)),
                  pl.BlockSpec((1, NSA_CMP_LEN, LANES, LANES), lambda i, j: (j, 0, 0, 0)),
                  pl.BlockSpec((NSA_CMP_LEN, LANES), lambda i, j: (0, 0))],
        out_specs=pl.BlockSpec((1, 1, nc, LANES), lambda i, j: (i, j, 0, 0)),
        out_shape=jax.ShapeDtypeStruct((b, 2, nc, LANES), BF16),
        compiler_params=_cparams(("arbitrary", "arbitrary")),
        name="nsa_compress",
    )(f32_proj, w, pe2)


def _rowmax(s):
    mx = s[:, :LANES]
    for c in range(LANES, s.shape[1], LANES):
        mx = jnp.maximum(mx, s[:, c:c + LANES])
    return jnp.broadcast_to(jnp.max(mx, axis=1, keepdims=True), (s.shape[0], LANES))


def _exp_sub(s, m):
    return jnp.concatenate([jnp.exp(s[:, c:c + LANES] - m).astype(BF16)
                            for c in range(0, s.shape[1], LANES)], axis=1)


def _flash_step(s, v_aug, m_ref, acc_ref):
    m_prev = m_ref[...]
    m_new = jnp.maximum(m_prev, _rowmax(s))
    acc_ref[...] = jnp.exp(m_prev - m_new) * acc_ref[...] + _dot(_exp_sub(s, m_new), v_aug)
    m_ref[...] = m_new


def _normalize(acc):
    return acc / pltpu.roll(acc, HEAD_DIM, axis=1)


def _head_slab(q_ref, h, g):
    slab = q_ref[0, :, LANES * (h // 2):LANES * (h // 2) + LANES].astype(F32) * SCALE
    if (h % 2) != g:
        slab = pltpu.roll(slab, HEAD_DIM, axis=1)
    return slab


def _stack_heads(q_ref, g, nq):
    lane = lax.broadcasted_iota(jnp.int32, (nq, LANES), 1)
    in_data = (lane // HEAD_DIM) == g
    return jnp.concatenate(
        [jnp.where(in_data, _head_slab(q_ref, GROUP * g + r, g), 0.0) for r in range(GROUP)], axis=0)


def _unstack_heads(o, g, nq):
    lane = lax.broadcasted_iota(jnp.int32, (nq, LANES), 1)
    slabs = []
    for m in range(GROUP // 2):
        even = o[(2 * m) * nq:(2 * m + 1) * nq]
        odd = o[(2 * m + 1) * nq:(2 * m + 2) * nq]
        if g == 0:
            odd = pltpu.roll(odd, HEAD_DIM, axis=1)
        else:
            even = pltpu.roll(even, HEAD_DIM, axis=1)
        slabs.append(jnp.where(lane < HEAD_DIM, even, odd))
    return jnp.concatenate(slabs, axis=1)


def _fill_aug(src_ref, dst_ref, block_len, row0=0):
    s = src_ref.shape[1]
    ch = min(512, s)
    for g in range(KV_HEADS):
        for c in range(s // ch):
            lane = lax.broadcasted_iota(jnp.int32, (ch, LANES), 1)
            in_data = (lane // HEAD_DIM) == g
            if block_len:
                key = lax.broadcasted_iota(jnp.int32, (ch, LANES), 0) + c * ch
                aux = (lane == (HEAD_DIM * (1 - g) + key // block_len)).astype(BF16)
            else:
                aux = jnp.ones((ch, LANES), BF16)
            dst_ref[g, row0 + c * ch:row0 + (c + 1) * ch, :] = jnp.where(
                in_data, src_ref[0, c * ch:(c + 1) * ch, :], aux)


def _pick_top(work, count):
    lane_f = lax.broadcasted_iota(jnp.int32, work.shape, 1).astype(F32)
    picked = jnp.zeros(work.shape, jnp.bool_)
    picked_real = jnp.zeros(work.shape, jnp.bool_)
    for _ in range(count):
        mx = jnp.max(work, axis=1, keepdims=True)
        idx = jnp.min(jnp.where(work == mx, lane_f, float(LANES)), axis=1, keepdims=True)
        pick = lane_f == idx
        picked = picked | pick
        picked_real = picked_real | (pick & (mx > 0.5 * NEG))
        work = jnp.where(pick, BELOW_NEG, work)
    return picked, picked_real


def _with_aux(qz, aux, g):
    lane = lax.broadcasted_iota(jnp.int32, qz.shape, 1)
    return jnp.where((lane // HEAD_DIM) == g, qz, aux).astype(BF16)


def _band_scores(qz, k_pad_ref, t_ref, g, qi, nq, window):
    span = pl.ds(pl.multiple_of(qi * nq, LANES), window + nq)
    s = _dot_nt(qz, k_pad_ref[span, :]) + t_ref[g]
    col = lax.broadcasted_iota(jnp.int32, s.shape, 1)
    return jnp.where(col < window - qi * nq, NEG, s), span


def _sweep(q_far_ref, q_near_ref, kaug_ref, vaug_ref, tnear_ref, m_ref, acc_ref, g, qi, nq):
    m_ref[...] = jnp.full_like(m_ref, NEG)
    acc_ref[...] = jnp.zeros_like(acc_ref)

    def far_step(t, carry):
        span = pl.ds(pl.multiple_of(t * FAR_TILE, FAR_TILE), FAR_TILE)
        _flash_step(_dot_nt(q_far_ref[g], kaug_ref[g, span, :]), vaug_ref[g, span, :], m_ref, acc_ref)
        return carry

    far_keys = jnp.maximum(qi - 1, 0) * nq
    lax.fori_loop(0, (far_keys + FAR_TILE - 1) // FAR_TILE, far_step, 0)
    span = pl.ds(pl.multiple_of(jnp.maximum(qi - 1, 0) * nq, nq), 2 * nq)
    _flash_step(_dot_nt(q_near_ref[g], kaug_ref[g, span, :]) + tnear_ref[jnp.minimum(qi, 1), g],
                vaug_ref[g, span, :], m_ref, acc_ref)
    return _normalize(acc_ref[...])


def _nsa_swa_kernel(qa_ref, qb_ref, ksel_ref, vsel_ref, kwin_ref, vwin_ref, kb_ref, vb_ref,
                    gates_ref, kc_ref, vc_ref, tc_ref, tnear_ref, twin_ref, tswa_ref,
                    ovw_ref, sink_ref, o_ref,
                    kaug_ref, vsel_aug_ref, kwin_pad_ref, vwin_aug_ref, kb_pad_ref, vb_aug_ref,
                    q_far_ref, q_near_ref, part_ref, m_ref, acc_ref):
    qi = pl.program_id(1)
    nq = NSA_Q
    rows = GROUP * nq
    nc = kc_ref.shape[2]

    @pl.when(qi == 0)
    def _():
        _fill_aug(ksel_ref, kaug_ref, NSA_SEL_LEN)
        _fill_aug(vsel_ref, vsel_aug_ref, 0)
        kwin_pad_ref[:NSA_WINDOW, :] = jnp.zeros((NSA_WINDOW, LANES), BF16)
        kwin_pad_ref[NSA_WINDOW:, :] = kwin_ref[0]
        vwin_aug_ref[:, :NSA_WINDOW, :] = jnp.zeros((KV_HEADS, NSA_WINDOW, LANES), BF16)
        _fill_aug(vwin_ref, vwin_aug_ref, 0, row0=NSA_WINDOW)
        kb_pad_ref[:SWA_WINDOW, :] = jnp.zeros((SWA_WINDOW, LANES), BF16)
        kb_pad_ref[SWA_WINDOW:, :] = kb_ref[0]
        vb_aug_ref[:, :SWA_WINDOW, :] = jnp.zeros((KV_HEADS, SWA_WINDOW, LANES), BF16)
        _fill_aug(vb_ref, vb_aug_ref, 0, row0=SWA_WINDOW)

    lane = lax.broadcasted_iota(jnp.int32, (nq, LANES), 1)
    ql = lax.broadcasted_iota(jnp.int32, (nq, LANES), 0)
    gates = gates_ref[0]

    def gate_col(g, branch):
        cols = [jnp.broadcast_to(gates[:, 3 * (GROUP * g + r) + branch:3 * (GROUP * g + r) + branch + 1],
                                 (nq, LANES)) for r in range(GROUP)]
        return jnp.concatenate(cols, axis=0)

    per_blk = nq // NSA_CMP_STRIDE
    u_id = lax.broadcasted_iota(jnp.int32, (LANES, nc), 0)
    n_id = lax.broadcasted_iota(jnp.int32, (LANES, nc), 1)
    shift = ((u_id < 2 * per_blk) & (n_id == per_blk * (qi - 1) + u_id)).astype(BF16)
    cmp_future = lax.broadcasted_iota(jnp.int32, (rows, nc), 1) >= per_blk * (qi + 1)

    works, qzs = [], []
    for g in range(KV_HEADS):
        qz = _stack_heads(qa_ref, g, nq)
        qz_b16 = qz.astype(BF16)
        qzs.append(qz)

        s = _dot_nt(qz_b16, kc_ref[0, 0]) + _dot(tc_ref[0, g], shift) + _dot(tc_ref[1, g], shift)
        s = jnp.where(cmp_future, NEG, s)
        m = jnp.max(s, axis=1, keepdims=True)
        e = jnp.exp(s - m)
        p = jnp.where(s > 0.5 * NEG, e, 0.0) / jnp.sum(e, axis=1, keepdims=True)
        o_cmp = _dot(p.astype(BF16), vc_ref[0, 0])

        p4 = p[0:nq] + p[nq:2 * nq] + p[2 * nq:3 * nq] + p[3 * nq:4 * nq]
        p_hi, p_lo = _split2(p4)
        imp = _dot(p_hi, ovw_ref[g]) + _dot(p_lo, ovw_ref[g])
        j = lane - HEAD_DIM * (1 - g)
        own = (nq // NSA_SEL_LEN) * qi + ql // NSA_SEL_LEN
        forced = (j == 0) | (j == own) | (j == own - 1)
        imp = jnp.where(j <= own, imp + jnp.where(forced, NSA_FORCE, 0.0), NEG)
        works.append(jnp.where((lane // HEAD_DIM) == (1 - g), imp, BELOW_NEG))

        s, span = _band_scores(qz_b16, kwin_pad_ref, twin_ref, g, qi, nq, NSA_WINDOW)
        o_win = _normalize(_dot(_exp_sub(s, _rowmax(s)), vwin_aug_ref[g, span, :]))
        part_ref[g] = gate_col(g, 0) * o_cmp + gate_col(g, 2) * o_win

        qz_b = _stack_heads(qb_ref, g, nq).astype(BF16)
        s, span = _band_scores(qz_b, kb_pad_ref, tswa_ref, g, qi, nq, SWA_WINDOW)
        sink = sink_ref[g]
        m = jnp.maximum(_rowmax(s), sink)
        r = _dot(_exp_sub(s, m), vb_aug_ref[g, span, :])
        o_b = r / (pltpu.roll(r, HEAD_DIM, axis=1) + jnp.exp(sink - m))
        o_ref[0, :, HQ + g * GROUP * HEAD_DIM:HQ + (g + 1) * GROUP * HEAD_DIM] = (
            _unstack_heads(o_b, g, nq).astype(BF16))

    picked, _ = _pick_top(jnp.concatenate(works, axis=0), NSA_SEL_TOP)
    far_blocks = (nq // NSA_SEL_LEN) * (qi - 1)
    for g in range(KV_HEADS):
        pk = picked[g * nq:(g + 1) * nq]
        j = lane - HEAD_DIM * (1 - g)
        sel_far = jnp.where(pk & (j < far_blocks), 0.0, MASK)
        sel_near = jnp.where(pk & (j >= far_blocks), 0.0, MASK)
        q_far_ref[g] = _with_aux(qzs[g], jnp.concatenate([sel_far] * GROUP, axis=0), g)
        q_near_ref[g] = _with_aux(qzs[g], jnp.concatenate([sel_near] * GROUP, axis=0), g)

    for g in range(KV_HEADS):
        o_sel = _sweep(q_far_ref, q_near_ref, kaug_ref, vsel_aug_ref, tnear_ref, m_ref, acc_ref, g, qi, nq)
        o_a = part_ref[g] + gate_col(g, 1) * o_sel
        o_ref[0, :, g * GROUP * HEAD_DIM:(g + 1) * GROUP * HEAD_DIM] = _unstack_heads(o_a, g, nq).astype(BF16)


def _nsa_swa(pb, pf, kvc, rel_table, sinks):
    b, s, _ = pb.shape
    nq = NSA_Q
    nc = s // NSA_CMP_STRIDE
    n_cmp = (s - NSA_CMP_LEN) // NSA_CMP_STRIDE + 1
    n_sel = s // NSA_SEL_LEN
    per_blk = nq // NSA_CMP_STRIDE
    assert n_sel <= HEAD_DIM and s % FAR_TILE == 0 and 2 * per_blk <= LANES
    h8 = HEADS_PER_MIXER
    f_a = _bias_by_dist(rel_table[:, :h8], NSA_WINDOW + nq)
    f_b = _bias_by_dist(rel_table[:, h8:2 * h8], NSA_WINDOW + nq)

    d_c = (jnp.arange(nq)[:, None] + nq - NSA_CMP_LEN + 1) - NSA_CMP_STRIDE * jnp.arange(2 * per_blk)[None, :]
    tc = jnp.where((d_c >= 0)[None], f_a[:, jnp.maximum(d_c, 0)], NEG)
    tc = jnp.pad(_rows_by_group(tc), ((0, 0), (0, 0), (0, LANES - 2 * per_blk)))
    tc_hi = tc.astype(BF16)
    tc = jnp.stack([tc_hi, (tc - tc_hi.astype(F32)).astype(BF16)])

    tnear = _near_tables(f_a, nq)
    twin = _band_table(f_a, nq, NSA_WINDOW)
    tswa = _band_table(f_b, nq, SWA_WINDOW)

    cmp_start = jnp.arange(nc) * NSA_CMP_STRIDE
    sel_start = jnp.arange(n_sel) * NSA_SEL_LEN
    overlap = jnp.maximum(jnp.minimum(cmp_start[:, None] + NSA_CMP_LEN, sel_start[None, :] + NSA_SEL_LEN)
                          - jnp.maximum(cmp_start[:, None], sel_start[None, :]), 0)
    overlap = jnp.where(jnp.arange(nc)[:, None] < n_cmp, overlap, 0).astype(F32) / NSA_CMP_LEN
    ovw = jnp.zeros((KV_HEADS, nc, LANES), F32)
    ovw = ovw.at[0, :, HEAD_DIM:HEAD_DIM + n_sel].set(overlap).at[1, :, :n_sel].set(overlap)
    ovw = ovw.astype(BF16)

    rows = GROUP * nq
    sink = (sinks.astype(F32) - rel_table[REL_BUCKETS - 1, h8:2 * h8]).reshape(KV_HEADS, GROUP, 1, 1)
    sink = jnp.broadcast_to(sink, (KV_HEADS, GROUP, nq, LANES)).reshape(KV_HEADS, rows, LANES)

    kv = lambda c: pl.BlockSpec((1, s, LANES), lambda i, j: (i, 0, c), pipeline_mode=pl.Buffered(1))
    aug = pltpu.VMEM((KV_HEADS, s, LANES), BF16)
    qsel = pltpu.VMEM((KV_HEADS, rows, LANES), BF16)
    return pl.pallas_call(
        _nsa_swa_kernel,
        grid=(b, s // nq),
        in_specs=[pl.BlockSpec((1, nq, HQ), lambda i, j: (i, j, 0)),
                  pl.BlockSpec((1, nq, HQ), lambda i, j: (i, j, 1)),
                  kv(8), kv(9), kv(10), kv(11), kv(12), kv(13),
                  pl.BlockSpec((1, nq, LANES), lambda i, j: (i, j, 2)),
                  pl.BlockSpec((1, 1, nc, LANES), lambda i, j: (i, 0, 0, 0)),
                  pl.BlockSpec((1, 1, nc, LANES), lambda i, j: (i, 1, 0, 0)),
                  _whole(tc), _whole(tnear), _whole(twin), _whole(tswa), _whole(ovw), _whole(sink)],
        out_specs=pl.BlockSpec((1, nq, 2 * HQ), lambda i, j: (i, j, 0)),
        out_shape=jax.ShapeDtypeStruct((b, s, 2 * HQ), BF16),
        scratch_shapes=[aug, aug,
                        pltpu.VMEM((s + NSA_WINDOW, LANES), BF16),
                        pltpu.VMEM((KV_HEADS, s + NSA_WINDOW, LANES), BF16),
                        pltpu.VMEM((s + SWA_WINDOW, LANES), BF16),
                        pltpu.VMEM((KV_HEADS, s + SWA_WINDOW, LANES), BF16),
                        qsel, qsel,
                        pltpu.VMEM((KV_HEADS, rows, LANES), F32),
                        pltpu.VMEM((rows, LANES), F32), pltpu.VMEM((rows, LANES), F32)],
        compiler_params=_cparams(("arbitrary", "arbitrary")),
        name="nsa_swa",
    )(pb, pb, pb, pb, pb, pb, pb, pb, pf, kvc, kvc, tc, tnear, twin, tswa, ovw, sink)


def _moba_kernel(q_ref, k_ref, v_ref, tnear_ref, o_ref, kaug_ref, vaug_ref, kmx_ref,
                 q_far_ref, q_near_ref, m_ref, acc_ref):
    qi = pl.program_id(1)
    nq = MOBA_BLOCK
    rows = GROUP * nq
    nb = k_ref.shape[1] // nq

    @pl.when(qi == 0)
    def _():
        _fill_aug(k_ref, kaug_ref, MOBA_BLOCK)
        _fill_aug(v_ref, vaug_ref, 0)
        kmx_ref[...] = jnp.zeros_like(kmx_ref)
        lane = lax.broadcasted_iota(jnp.int32, (1, LANES), 1)
        for blk in range(nb):
            mean = jnp.mean(k_ref[0, blk * nq:(blk + 1) * nq, :].astype(F32), axis=0, keepdims=True)
            for g in range(KV_HEADS):
                hi, lo = _split2(jnp.where((lane // HEAD_DIM) == g, mean, 0.0))
                r = HEAD_DIM * (1 - g) + blk
                kmx_ref[0, g, r:r + 1, :] = hi
                kmx_ref[1, g, r:r + 1, :] = lo

    lane = lax.broadcasted_iota(jnp.int32, (rows, LANES), 1)
    works, qzs = [], []
    for g in range(KV_HEADS):
        qz = _stack_heads(q_ref, g, nq)
        qz_b16 = qz.astype(BF16)
        gs = _dot_nt(qz_b16, kmx_ref[0, g]) + _dot_nt(qz_b16, kmx_ref[1, g])
        j = lane - HEAD_DIM * (1 - g)
        works.append(jnp.where((j >= 0) & (j < qi), gs, BELOW_NEG))
        qzs.append(qz)
    _, picked = _pick_top(jnp.concatenate(works, axis=0), MOBA_TOP)
    for g in range(KV_HEADS):
        pk = picked[g * rows:(g + 1) * rows]
        j = lane - HEAD_DIM * (1 - g)
        q_far_ref[g] = _with_aux(qzs[g], jnp.where(pk & (j < qi - 1), 0.0, MASK), g)
        q_near_ref[g] = _with_aux(qzs[g], jnp.where((pk & (j >= qi - 1)) | (j == qi), 0.0, MASK), g)

    for g in range(KV_HEADS):
        o = _sweep(q_far_ref, q_near_ref, kaug_ref, vaug_ref, tnear_ref, m_ref, acc_ref, g, qi, nq)
        o_ref[0, :, g * GROUP * HEAD_DIM:(g + 1) * GROUP * HEAD_DIM] = _unstack_heads(o, g, nq).astype(BF16)


def _moba(pb, rel_table, q_col, k_col, v_col):
    b, s, _ = pb.shape
    nq = MOBA_BLOCK
    nb = s // nq
    assert s % FAR_TILE == 0 and nb >= 2 and nb <= HEAD_DIM
    tnear = _near_tables(_bias_by_dist(rel_table[:, :HEADS_PER_MIXER], 2 * nq), nq)
    rows = GROUP * nq
    aug = pltpu.VMEM((KV_HEADS, s, LANES), BF16)
    qsel = pltpu.VMEM((KV_HEADS, rows, LANES), BF16)
    kv = lambda c: pl.BlockSpec((1, s, LANES), lambda i, j: (i, 0, c), pipeline_mode=pl.Buffered(1))
    return pl.pallas_call(
        _moba_kernel,
        grid=(b, nb),
        in_specs=[pl.BlockSpec((1, nq, HQ), lambda i, j: (i, j, q_col)), kv(k_col), kv(v_col), _whole(tnear)],
        out_specs=pl.BlockSpec((1, nq, HQ), lambda i, j: (i, j, 0)),
        out_shape=jax.ShapeDtypeStruct((b, s, HQ), BF16),
        scratch_shapes=[aug, aug,
                        pltpu.VMEM((2, KV_HEADS, LANES, LANES), BF16),
                        qsel, qsel,
                        pltpu.VMEM((rows, LANES), F32), pltpu.VMEM((rows, LANES), F32)],
        compiler_params=_cparams(("arbitrary", "arbitrary")),
        name="moba",
    )(pb, pb, pb, tnear)


def _stick_kernel(q_ref, k_ref, v_ref, tri_ref, o_ref, acc_ref, c_ref):
    qi = pl.program_id(1)
    t_len = SB_T
    pairs = HEADS_PER_MIXER // 2
    lane = lax.broadcasted_iota(jnp.int32, (t_len, LANES), 1)
    strict = lax.broadcasted_iota(jnp.int32, (t_len, t_len), 1) < lax.broadcasted_iota(jnp.int32, (t_len, t_len), 0)

    def pair_tile(t, pair, diag):
        rows = pl.ds(pl.multiple_of(t * t_len, t_len), t_len)
        cols = slice(pair * LANES, (pair + 1) * LANES)
        q = q_ref[0, :, cols].astype(F32) * SCALE
        k = k_ref[0, rows, cols]
        v = v_ref[0, rows, cols]
        for half in range(2):
            h = 2 * pair + half
            z = _dot_nt(jnp.where((lane // HEAD_DIM) == half, q, 0.0).astype(BF16), k)
            log_1m = -(jnp.maximum(z, 0.0) + jnp.log(1.0 + jnp.exp(-jnp.abs(z))))
            if diag:
                log_1m = jnp.where(strict, log_1m, 0.0)
            sums = _dot(jnp.concatenate(_split2(log_1m), axis=1), tri_ref[...])
            c = jnp.zeros((t_len, LANES), F32) if diag else c_ref[h]
            after = sums[:, :t_len] + jnp.concatenate([c] * (t_len // LANES), axis=1)
            w = jnp.exp(log_1m + z + after)
            if diag:
                acc_ref[h] = _dot(jnp.where(strict, w, 0.0).astype(BF16), v)
            else:
                acc_ref[h] += _dot(w.astype(BF16), v)
            c_ref[h] = c + sums[:, t_len:]

    def c_max(pair):
        return jnp.max(c_ref[2 * pair:2 * pair + 2])

    for pair in range(pairs):
        pair_tile(qi, pair, True)

    def cond(carry):
        t, c_maxes = carry
        live = c_maxes[0] >= EXP_ZERO
        for pair in range(1, pairs):
            live = live | (c_maxes[pair] >= EXP_ZERO)
        return (t >= 0) & live

    def body(carry):
        t, c_maxes = carry
        for pair in range(pairs):
            pl.when(c_maxes[pair] >= EXP_ZERO)(functools.partial(pair_tile, t, pair, False))
        return t - 1, tuple(c_max(pair) for pair in range(pairs))

    lax.while_loop(cond, body, (qi - 1, tuple(c_max(pair) for pair in range(pairs))))
    for pair in range(pairs):
        o_ref[0, :, pair * LANES:(pair + 1) * LANES] = jnp.where(
            (lane // HEAD_DIM) == 0, acc_ref[2 * pair], acc_ref[2 * pair + 1]).astype(BF16)


def _stick_breaking(pb, q_col, k_col, v_col):
    b, s, _ = pb.shape
    t_len = SB_T
    assert s % t_len == 0
    tri = (jnp.arange(t_len)[:, None] > jnp.arange(t_len)[None, :]).astype(BF16)
    tri = jnp.concatenate([tri, jnp.ones((t_len, LANES), BF16)], axis=1)
    tri = jnp.concatenate([tri, tri], axis=0)
    return pl.pallas_call(
        _stick_kernel,
        grid=(b, s // t_len),
        in_specs=[pl.BlockSpec((1, t_len, HQ), lambda i, j: (i, j, q_col)),
                  pl.BlockSpec((1, s, HQ), lambda i, j: (i, 0, k_col)),
                  pl.BlockSpec((1, s, HQ), lambda i, j: (i, 0, v_col)),
                  _whole(tri)],
        out_specs=pl.BlockSpec((1, t_len, HQ), lambda i, j: (i, j, 0)),
        out_shape=jax.ShapeDtypeStruct((b, s, HQ), BF16),
        scratch_shapes=[pltpu.VMEM((HEADS_PER_MIXER, t_len, LANES), F32),
                        pltpu.VMEM((HEADS_PER_MIXER, t_len, LANES), F32)],
        compiler_params=_cparams(("arbitrary", "arbitrary")),
        name="stick_breaking",
    )(pb, pb, pb, tri)


def _split_cols(w, sizes):
    offs = [0]
    for z in sizes:
        offs.append(offs[-1] + z)
    return [w[:, offs[i]:offs[i + 1]] for i in range(len(sizes))]


def kernel(x, c, rel_table, mod_w, mod_b, norm_w, w_in_ab, w_out_ab, nsa_cmp_wk, nsa_cmp_wv,
           nsa_cmp_pe, swa_sinks, w_in_cd, w_out_cd, ffn_w_in, ffn_w_out):
    depth = mod_w.shape[0]
    d = x.shape[-1]
    h8 = HEADS_PER_MIXER
    mod = _modulation(c, mod_w.reshape(depth * 2, d, 3 * d), mod_b.reshape(depth * 2, 3 * d))
    mod = mod.reshape(depth, 2, c.shape[0], 3, d)

    for layer in range(depth):
        shift, scale, gate = mod[layer, 0, :, 0], mod[layer, 0, :, 1], mod[layer, 0, :, 2]
        i = layer // 2
        if layer % 2 == 0:
            qa, kca, vca, ksa, vsa, kwa, vwa, ga, qb, kb, vb = _split_cols(
                w_in_ab[i], [HQ, KVW, KVW, KVW, KVW, KVW, KVW, 3 * h8, HQ, KVW, KVW])
            ga = jnp.pad(ga, ((0, 0), (0, LANES - 3 * h8)))
            w = jnp.concatenate([qa, qb, ksa, vsa, kwa, vwa, kb, vb, kca, vca, ga], axis=1).astype(BF16)
            nb = 2 * HQ + 6 * KVW
            pb, pf = _in_proj(x, norm_w[layer, 0, 0], scale, shift, w, nb, 3 * LANES, LANES)
            kvc = _compress(pf, nsa_cmp_wk[i], nsa_cmp_wv[i], nsa_cmp_pe[i])
            o = _nsa_swa(pb, pf, kvc, rel_table, swa_sinks[i])
            x = _out_proj(o, 0, o, 1, w_out_ab[i].astype(BF16), x, norm_w[layer, 0, 1], gate)
        else:
            qc, kc, vc, qd, kd, vd = _split_cols(w_in_cd[i], [HQ, KVW, KVW, HQ, HQ, HQ])
            w = jnp.concatenate([qc, qd, kd, vd, kc, vc], axis=1).astype(BF16)
            (pb,) = _in_proj(x, norm_w[layer, 0, 0], scale, shift, w, 4 * HQ + 2 * KVW, 0, 0)
            o_c = _moba(pb, rel_table, 0, 4 * HQ // LANES, 4 * HQ // LANES + 1)
            o_d = _stick_breaking(pb, 1, 2, 3)
            x = _out_proj(o_c, 0, o_d, 0, w_out_cd[i].astype(BF16), x, norm_w[layer, 0, 1], gate)

        shift, scale, gate = mod[layer, 1, :, 0], mod[layer, 1, :, 1], mod[layer, 1, :, 2]
        x = _ffn(x, norm_w[layer, 1, 0], scale, shift, ffn_w_in[layer].astype(BF16),
                 ffn_w_out[layer].astype(BF16), norm_w[layer, 1, 1], gate)
    return x
```

```python
import functools
import math

import jax
import jax.numpy as jnp
from jax import lax
from jax.experimental import pallas as pl
from jax.experimental.pallas import tpu as pltpu

F32 = jnp.float32
BF16 = jnp.bfloat16

HEAD_DIM = 64
LANES = 128
KV_HEADS = 2
GROUP = 4
HEADS_PER_MIXER = KV_HEADS * GROUP
HQ = HEADS_PER_MIXER * HEAD_DIM
KVW = KV_HEADS * HEAD_DIM
NSA_Q = 256
NSA_CMP_LEN = 32
NSA_CMP_STRIDE = 16
NSA_SEL_LEN = 64
NSA_SEL_TOP = 8
NSA_WINDOW = 512
NSA_FORCE = 1e4
SWA_WINDOW = 128
MOBA_BLOCK = 256
MOBA_TOP = 3
REL_BUCKETS = 32
REL_MAX_DIST = 128
RMS_EPS = 1e-6
NEG = -1e30
MASK = -(2.0 ** 100)
BELOW_NEG = -3e38
SCALE = HEAD_DIM ** -0.5
FAR_TILE = 512
SB_T = 256
EXP_ZERO = -104.0
VMEM_LIMIT = 60 * 1024 * 1024

_NT = (((1,), (1,)), ((), ()))


def _cparams(sem):
    return pltpu.CompilerParams(dimension_semantics=sem, vmem_limit_bytes=VMEM_LIMIT)


def _dot(a, b):
    return jnp.dot(a, b, preferred_element_type=F32)


def _dot_nt(a, b):
    return lax.dot_general(a, b, _NT, preferred_element_type=F32)


def _split2(x):
    hi = x.astype(BF16)
    lo = (x - hi.astype(F32)).astype(BF16)
    return hi, lo


def _rms(x, w):
    return x * lax.rsqrt(jnp.mean(x * x, axis=-1, keepdims=True) + RMS_EPS) * w


def _whole(a):
    return pl.BlockSpec(a.shape, lambda *_: (0,) * a.ndim)


def _mod_kernel(c_ref, w_ref, b_ref, o_ref):
    o_ref[0] = jnp.dot(c_ref[...], w_ref[0], preferred_element_type=F32,
                       precision=lax.Precision.HIGHEST) + b_ref[0]


def _modulation(c, mod_w, mod_b):
    n, d, d3 = mod_w.shape
    b = c.shape[0]
    tn = 1024
    return pl.pallas_call(
        _mod_kernel,
        grid=(n, d3 // tn),
        in_specs=[pl.BlockSpec((b, d), lambda i, j: (0, 0)),
                  pl.BlockSpec((1, d, tn), lambda i, j: (i, 0, j)),
                  pl.BlockSpec((1, 1, tn), lambda i, j: (i, 0, j))],
        out_specs=pl.BlockSpec((1, b, tn), lambda i, j: (i, 0, j)),
        out_shape=jax.ShapeDtypeStruct((n, b, d3), F32),
        compiler_params=_cparams(("arbitrary", "arbitrary")),
        name="modulation",
    )(c, mod_w, mod_b.reshape(n, 1, d3))


def _in_proj_kernel(x_ref, nw_ref, sc_ref, sh_ref, w_ref, *out_refs, nb, nf, gate_cols):
    x = x_ref[0]
    h = _rms(x, nw_ref[...]) * (1.0 + sc_ref[0]) + sh_ref[0]
    hb = h.astype(BF16)
    ob_ref = out_refs[0]
    for c0 in range(0, nb, 512):
        c1 = min(c0 + 512, nb)
        ob_ref[0, :, c0:c1] = _dot(hb, w_ref[:, c0:c1]).astype(BF16)
    if nf:
        of_ref = out_refs[1]
        y = _dot(hb, w_ref[:, nb:nb + nf])
        if gate_cols:
            y_g = jax.nn.sigmoid(y[:, nf - gate_cols:])
            of_ref[0, :, :nf - gate_cols] = y[:, :nf - gate_cols]
            of_ref[0, :, nf - gate_cols:] = y_g
        else:
            of_ref[0] = y


def _in_proj(x, norm_w, scale, shift, w, nb, nf, gate_cols, tm=512):
    b, s, d = x.shape
    tm = min(tm, s)
    out_shape = [jax.ShapeDtypeStruct((b, s, nb), BF16)]
    out_specs = [pl.BlockSpec((1, tm, nb), lambda i, j: (i, j, 0))]
    if nf:
        out_shape.append(jax.ShapeDtypeStruct((b, s, nf), F32))
        out_specs.append(pl.BlockSpec((1, tm, nf), lambda i, j: (i, j, 0)))
    return pl.pallas_call(
        functools.partial(_in_proj_kernel, nb=nb, nf=nf, gate_cols=gate_cols),
        grid=(b, s // tm),
        in_specs=[pl.BlockSpec((1, tm, d), lambda i, j: (i, j, 0)),
                  pl.BlockSpec((1, d), lambda i, j: (0, 0)),
                  pl.BlockSpec((1, 1, d), lambda i, j: (i, 0, 0)),
                  pl.BlockSpec((1, 1, d), lambda i, j: (i, 0, 0)),
                  pl.BlockSpec((d, nb + nf), lambda i, j: (0, 0))],
        out_specs=out_specs,
        out_shape=out_shape,
        compiler_params=_cparams(("arbitrary", "arbitrary")),
        name="in_proj",
    )(x, norm_w.reshape(1, d), scale.reshape(b, 1, d), shift.reshape(b, 1, d), w)


def _mix_ffn_kernel(o1_ref, o2_ref, wm1_ref, wm2_ref, x_ref, nwm_ref, gm_ref,
                    nw1_ref, sc_ref, sh_ref, wi_ref, wo_ref, nw2_ref, gate_ref, xo_ref, *, tf):
    dff = wo_ref.shape[0]
    y = _dot(o1_ref[0], wm1_ref[...]) + _dot(o2_ref[0], wm2_ref[...])
    x = x_ref[0] + gm_ref[0] * _rms(y, nwm_ref[...])
    hb = (_rms(x, nw1_ref[...]) * (1.0 + sc_ref[0]) + sh_ref[0]).astype(BF16)
    y = None
    for c in range(0, dff, tf):
        g = _dot(hb, wi_ref[:, c:c + tf])
        u = _dot(hb, wi_ref[:, dff + c:dff + c + tf])
        a = (g * jax.nn.sigmoid(g) * u).astype(BF16)
        part = _dot(a, wo_ref[c:c + tf, :])
        y = part if y is None else y + part
    xo_ref[0] = x + gate_ref[0] * _rms(y, nw2_ref[...])


def _mix_ffn(o1, c1, o2, c2, w_mix, x, nw_mix, gate_mix, nw1, scale, shift, w_in, w_out, nw2, gate,
             tm=512, tf=1408):
    b, s, d = x.shape
    dff = w_out.shape[0]
    tm = min(tm, s)
    assert dff % tf == 0 and tf % LANES == 0
    vec = lambda a: a.reshape(b, 1, d)
    row = pl.BlockSpec((1, d), lambda i, j: (0, 0))
    per_batch = pl.BlockSpec((1, 1, d), lambda i, j: (i, 0, 0))
    return pl.pallas_call(
        functools.partial(_mix_ffn_kernel, tf=tf),
        grid=(b, s // tm),
        in_specs=[pl.BlockSpec((1, tm, HQ), lambda i, j: (i, j, c1)),
                  pl.BlockSpec((1, tm, HQ), lambda i, j: (i, j, c2)),
                  pl.BlockSpec((HQ, d), lambda i, j: (0, 0)),
                  pl.BlockSpec((HQ, d), lambda i, j: (1, 0)),
                  pl.BlockSpec((1, tm, d), lambda i, j: (i, j, 0)),
                  row, per_batch, row, per_batch, per_batch,
                  _whole(w_in), _whole(w_out), row, per_batch],
        out_specs=pl.BlockSpec((1, tm, d), lambda i, j: (i, j, 0)),
        out_shape=jax.ShapeDtypeStruct((b, s, d), F32),
        compiler_params=_cparams(("arbitrary", "arbitrary")),
        name="mix_ffn",
    )(o1, o2, w_mix, w_mix, x, nw_mix.reshape(1, d), vec(gate_mix),
      nw1.reshape(1, d), vec(scale), vec(shift), w_in, w_out, nw2.reshape(1, d), vec(gate))


def _rel_bucket(dist):
    n = jnp.maximum(dist, 0)
    max_exact = REL_BUCKETS // 2
    nf = jnp.maximum(n, 1).astype(jnp.float32)
    large = max_exact + (jnp.log(nf / max_exact) / math.log(REL_MAX_DIST / max_exact)
                         * (REL_BUCKETS - max_exact)).astype(jnp.int32)
    large = jnp.minimum(large, REL_BUCKETS - 1)
    return jnp.where(n < max_exact, n, large)


def _bias_by_dist(tab, n):
    return (tab[_rel_bucket(jnp.arange(n))] - tab[REL_BUCKETS - 1]).T.astype(F32)


def _toeplitz(f, nq, nk, offset):
    p = nq + nk - 1
    k = jnp.arange(p)
    delta = jnp.where(k < nk, k, k - p)
    w = f[:, jnp.clip(offset - delta, 0, f.shape[1] - 1)]
    flat = jnp.tile(w, (1, nq))[:, :nq * (p - 1)]
    return flat.reshape(f.shape[0], nq, p - 1)[:, :, :nk]


def _rows_by_group(t):
    h, nq, nk = t.shape
    return t.reshape(KV_HEADS, GROUP * nq, nk)


def _dist(nq, nk, offset):
    return (jnp.arange(nq)[:, None] + offset) - jnp.arange(nk)[None, :]


def _masked(t, valid):
    return jnp.where(valid[None], t, NEG)


def _band_table(f, nq, window):
    d = _dist(nq, window + nq, window)
    return _rows_by_group(_masked(_toeplitz(f, nq, window + nq, window), (d >= 0) & (d < window)))


def _near_tables(f, nq):
    d_first = _dist(nq, 2 * nq, 0)
    d_near = _dist(nq, 2 * nq, nq)
    return jnp.stack([_rows_by_group(_masked(_toeplitz(f, nq, 2 * nq, 0), d_first >= 0)),
                      _rows_by_group(_masked(_toeplitz(f, nq, 2 * nq, nq), d_near >= 0))])


def _compress_kernel(k_ref, w_ref, pe_ref, o_ref):
    nc = o_ref.shape[2]
    half = NSA_CMP_LEN // 2
    lo = jnp.zeros((nc, LANES), F32)
    hi = jnp.zeros((nc, LANES), F32)
    for l in range(half):
        rows = k_ref[0, pl.ds(l, nc, stride=NSA_CMP_STRIDE), :]
        lo = lo + _dot((rows + pe_ref[l:l + 1, :]).astype(BF16), w_ref[0, l])
        hi = hi + _dot((rows + pe_ref[half + l:half + l + 1, :]).astype(BF16), w_ref[0, half + l])
    o_ref[0, 0] = (lo + pltpu.roll(hi, nc - 1, axis=0)).astype(BF16)


def _compress(f32_proj, wk, wv, pe):
    b, s, _ = f32_proj.shape
    nc = s // NSA_CMP_STRIDE
    eye = jnp.eye(KV_HEADS, dtype=F32)
    bd = lambda w: jnp.einsum('gh,lde->lgdhe', eye, w).reshape(NSA_CMP_LEN, LANES, LANES)
    w = jnp.stack([bd(wk), bd(wv)]).astype(BF16)
    pe2 = jnp.tile(pe, (1, KV_HEADS)).astype(F32)
    return pl.pallas_call(
        _compress_kernel,
        grid=(b, 2),
        in_specs=[pl.BlockSpec((1, s, LANES), lambda i, j: (i, 0, j)),
                  pl.BlockSpec((1, NSA_CMP_LEN, LANES, LANES), lambda i, j: (j, 0, 0, 0)),
                  pl.BlockSpec((NSA_CMP_LEN, LANES), lambda i, j: (0, 0))],
        out_specs=pl.BlockSpec((1, 1, nc, LANES), lambda i, j: (i, j, 0, 0)),
        out_shape=jax.ShapeDtypeStruct((b, 2, nc, LANES), BF16),
        compiler_params=_cparams(("arbitrary", "arbitrary")),
        name="nsa_compress",
    )(f32_proj, w, pe2)


def _rowmax(s):
    mx = s[:, :LANES]
    for c in range(LANES, s.shape[1], LANES):
        mx = jnp.maximum(mx, s[:, c:c + LANES])
    return jnp.broadcast_to(jnp.max(mx, axis=1, keepdims=True), (s.shape[0], LANES))


def _exp_sub(s, m):
    return jnp.concatenate([jnp.exp(s[:, c:c + LANES] - m).astype(BF16)
                            for c in range(0, s.shape[1], LANES)], axis=1)


def _flash_step(scores, v_augs, m_ref, acc_ref):
    groups = range(len(scores))
    m_prev = [m_ref[g] for g in groups]
    m_new = [jnp.maximum(m_prev[g], _rowmax(scores[g])) for g in groups]
    p = [_exp_sub(scores[g], m_new[g]) for g in groups]
    for g in groups:
        acc_ref[g] = jnp.exp(m_prev[g] - m_new[g]) * acc_ref[g] + _dot(p[g], v_augs[g])
        m_ref[g] = m_new[g]


def _normalize(acc):
    return acc / pltpu.roll(acc, HEAD_DIM, axis=1)


def _head_slab(q_ref, h, g):
    slab = q_ref[0, :, LANES * (h // 2):LANES * (h // 2) + LANES].astype(F32) * SCALE
    if (h % 2) != g:
        slab = pltpu.roll(slab, HEAD_DIM, axis=1)
    return slab


def _stack_heads(q_ref, g, nq):
    lane = lax.broadcasted_iota(jnp.int32, (nq, LANES), 1)
    in_data = (lane // HEAD_DIM) == g
    return jnp.concatenate(
        [jnp.where(in_data, _head_slab(q_ref, GROUP * g + r, g), 0.0) for r in range(GROUP)], axis=0)


def _unstack_heads(o, g, nq):
    lane = lax.broadcasted_iota(jnp.int32, (nq, LANES), 1)
    slabs = []
    for m in range(GROUP // 2):
        even = o[(2 * m) * nq:(2 * m + 1) * nq]
        odd = o[(2 * m + 1) * nq:(2 * m + 2) * nq]
        if g == 0:
            odd = pltpu.roll(odd, HEAD_DIM, axis=1)
        else:
            even = pltpu.roll(even, HEAD_DIM, axis=1)
        slabs.append(jnp.where(lane < HEAD_DIM, even, odd))
    return jnp.concatenate(slabs, axis=1)


def _fill_aug(src_ref, dst_ref, block_len, row0=0):
    s = src_ref.shape[1]
    ch = min(512, s)
    for g in range(KV_HEADS):
        for c in range(s // ch):
            lane = lax.broadcasted_iota(jnp.int32, (ch, LANES), 1)
            in_data = (lane // HEAD_DIM) == g
            if block_len:
                key = lax.broadcasted_iota(jnp.int32, (ch, LANES), 0) + c * ch
                aux = (lane == (HEAD_DIM * (1 - g) + key // block_len)).astype(BF16)
            else:
                aux = jnp.ones((ch, LANES), BF16)
            dst_ref[g, row0 + c * ch:row0 + (c + 1) * ch, :] = jnp.where(
                in_data, src_ref[0, c * ch:(c + 1) * ch, :], aux)


def _pick_top(work, count):
    lane_f = lax.broadcasted_iota(jnp.int32, work.shape, 1).astype(F32)
    picked = jnp.zeros(work.shape, jnp.bool_)
    picked_real = jnp.zeros(work.shape, jnp.bool_)
    for _ in range(count):
        mx = jnp.max(work, axis=1, keepdims=True)
        idx = jnp.min(jnp.where(work == mx, lane_f, float(LANES)), axis=1, keepdims=True)
        pick = lane_f == idx
        picked = picked | pick
        picked_real = picked_real | (pick & (mx > 0.5 * NEG))
        work = jnp.where(pick, BELOW_NEG, work)
    return picked, picked_real


def _with_aux(qz, aux, g):
    lane = lax.broadcasted_iota(jnp.int32, qz.shape, 1)
    return jnp.where((lane // HEAD_DIM) == g, qz, aux).astype(BF16)


def _band_scores(qz, k_pad_ref, t_ref, g, qi, nq, window):
    span = pl.ds(pl.multiple_of(qi * nq, LANES), window + nq)
    s = _dot_nt(qz, k_pad_ref[span, :]) + t_ref[g]
    col = lax.broadcasted_iota(jnp.int32, s.shape, 1)
    return jnp.where(col < window - qi * nq, NEG, s), span


def _sweep(q_far_ref, q_near_ref, kaug_ref, vaug_ref, tnear_ref, m_ref, acc_ref, qi, nq):
    groups = range(KV_HEADS)
    m_ref[...] = jnp.full_like(m_ref, NEG)
    acc_ref[...] = jnp.zeros_like(acc_ref)

    def far_step(t, carry):
        span = pl.ds(pl.multiple_of(t * FAR_TILE, FAR_TILE), FAR_TILE)
        _flash_step([_dot_nt(q_far_ref[g], kaug_ref[g, span, :]) for g in groups],
                    [vaug_ref[g, span, :] for g in groups], m_ref, acc_ref)
        return carry

    far_keys = jnp.maximum(qi - 1, 0) * nq
    lax.fori_loop(0, (far_keys + FAR_TILE - 1) // FAR_TILE, far_step, 0)
    span = pl.ds(pl.multiple_of(jnp.maximum(qi - 1, 0) * nq, nq), 2 * nq)
    first = jnp.minimum(qi, 1)
    _flash_step([_dot_nt(q_near_ref[g], kaug_ref[g, span, :]) + tnear_ref[first, g] for g in groups],
                [vaug_ref[g, span, :] for g in groups], m_ref, acc_ref)
    return [_normalize(acc_ref[g]) for g in groups]


def _nsa_swa_kernel(qa_ref, qb_ref, ksel_ref, vsel_ref, kwin_ref, vwin_ref, kb_ref, vb_ref,
                    gates_ref, kc_ref, vc_ref, tc_ref, tnear_ref, twin_ref, tswa_ref,
                    ovw_ref, sink_ref, o_ref,
                    kaug_ref, vsel_aug_ref, kwin_pad_ref, vwin_aug_ref, kb_pad_ref, vb_aug_ref,
                    q_far_ref, q_near_ref, part_ref, m_ref, acc_ref):
    qi = pl.program_id(1)
    nq = NSA_Q
    rows = GROUP * nq
    nc = kc_ref.shape[2]

    @pl.when(qi == 0)
    def _():
        _fill_aug(ksel_ref, kaug_ref, NSA_SEL_LEN)
        _fill_aug(vsel_ref, vsel_aug_ref, 0)
        kwin_pad_ref[:NSA_WINDOW, :] = jnp.zeros((NSA_WINDOW, LANES), BF16)
        kwin_pad_ref[NSA_WINDOW:, :] = kwin_ref[0]
        vwin_aug_ref[:, :NSA_WINDOW, :] = jnp.zeros((KV_HEADS, NSA_WINDOW, LANES), BF16)
        _fill_aug(vwin_ref, vwin_aug_ref, 0, row0=NSA_WINDOW)
        kb_pad_ref[:SWA_WINDOW, :] = jnp.zeros((SWA_WINDOW, LANES), BF16)
        kb_pad_ref[SWA_WINDOW:, :] = kb_ref[0]
        vb_aug_ref[:, :SWA_WINDOW, :] = jnp.zeros((KV_HEADS, SWA_WINDOW, LANES), BF16)
        _fill_aug(vb_ref, vb_aug_ref, 0, row0=SWA_WINDOW)

    lane = lax.broadcasted_iota(jnp.int32, (nq, LANES), 1)
    ql = lax.broadcasted_iota(jnp.int32, (nq, LANES), 0)
    gates = gates_ref[0]

    def gate_col(g, branch):
        cols = [jnp.broadcast_to(gates[:, 3 * (GROUP * g + r) + branch:3 * (GROUP * g + r) + branch + 1],
                                 (nq, LANES)) for r in range(GROUP)]
        return jnp.concatenate(cols, axis=0)

    per_blk = nq // NSA_CMP_STRIDE
    u_id = lax.broadcasted_iota(jnp.int32, (LANES, nc), 0)
    n_id = lax.broadcasted_iota(jnp.int32, (LANES, nc), 1)
    shift = ((u_id < 2 * per_blk) & (n_id == per_blk * (qi - 1) + u_id)).astype(BF16)
    cmp_future = lax.broadcasted_iota(jnp.int32, (rows, nc), 1) >= per_blk * (qi + 1)

    groups = range(KV_HEADS)
    qzs = [_stack_heads(qa_ref, g, nq) for g in groups]
    qz_b16 = [qz.astype(BF16) for qz in qzs]

    s = [_dot_nt(qz_b16[g], kc_ref[0, 0]) + _dot(tc_ref[0, g], shift) + _dot(tc_ref[1, g], shift)
         for g in groups]
    s = [jnp.where(cmp_future, NEG, s[g]) for g in groups]
    e = [jnp.exp(s[g] - jnp.max(s[g], axis=1, keepdims=True)) for g in groups]
    p = [jnp.where(s[g] > 0.5 * NEG, e[g], 0.0) / jnp.sum(e[g], axis=1, keepdims=True) for g in groups]
    o_cmp = [_dot(p[g].astype(BF16), vc_ref[0, 0]) for g in groups]

    works = []
    for g in groups:
        p4 = p[g][0:nq] + p[g][nq:2 * nq] + p[g][2 * nq:3 * nq] + p[g][3 * nq:4 * nq]
        p_hi, p_lo = _split2(p4)
        imp = _dot(p_hi, ovw_ref[g]) + _dot(p_lo, ovw_ref[g])
        j = lane - HEAD_DIM * (1 - g)
        own = (nq // NSA_SEL_LEN) * qi + ql // NSA_SEL_LEN
        forced = (j == 0) | (j == own) | (j == own - 1)
        imp = jnp.where(j <= own, imp + jnp.where(forced, NSA_FORCE, 0.0), NEG)
        works.append(jnp.where((lane // HEAD_DIM) == (1 - g), imp, BELOW_NEG))

    band = [_band_scores(qz_b16[g], kwin_pad_ref, twin_ref, g, qi, nq, NSA_WINDOW) for g in groups]
    pw = [_exp_sub(s_w, _rowmax(s_w)) for s_w, _ in band]
    for g in groups:
        o_win = _normalize(_dot(pw[g], vwin_aug_ref[g, band[g][1], :]))
        part_ref[g] = gate_col(g, 0) * o_cmp[g] + gate_col(g, 2) * o_win

    qz_b = [_stack_heads(qb_ref, g, nq).astype(BF16) for g in groups]
    band = [_band_scores(qz_b[g], kb_pad_ref, tswa_ref, g, qi, nq, SWA_WINDOW) for g in groups]
    m_b = [jnp.maximum(_rowmax(band[g][0]), sink_ref[g]) for g in groups]
    pb = [_exp_sub(band[g][0], m_b[g]) for g in groups]
    for g in groups:
        r = _dot(pb[g], vb_aug_ref[g, band[g][1], :])
        o_b = r / (pltpu.roll(r, HEAD_DIM, axis=1) + jnp.exp(sink_ref[g] - m_b[g]))
        o_ref[0, :, HQ + g * GROUP * HEAD_DIM:HQ + (g + 1) * GROUP * HEAD_DIM] = (
            _unstack_heads(o_b, g, nq).astype(BF16))

    picked, _ = _pick_top(jnp.concatenate(works, axis=0), NSA_SEL_TOP)
    far_blocks = (nq // NSA_SEL_LEN) * (qi - 1)
    for g in range(KV_HEADS):
        pk = picked[g * nq:(g + 1) * nq]
        j = lane - HEAD_DIM * (1 - g)
        sel_far = jnp.where(pk & (j < far_blocks), 0.0, MASK)
        sel_near = jnp.where(pk & (j >= far_blocks), 0.0, MASK)
        q_far_ref[g] = _with_aux(qzs[g], jnp.concatenate([sel_far] * GROUP, axis=0), g)
        q_near_ref[g] = _with_aux(qzs[g], jnp.concatenate([sel_near] * GROUP, axis=0), g)

    o_sel = _sweep(q_far_ref, q_near_ref, kaug_ref, vsel_aug_ref, tnear_ref, m_ref, acc_ref, qi, nq)
    for g in groups:
        o_a = part_ref[g] + gate_col(g, 1) * o_sel[g]
        o_ref[0, :, g * GROUP * HEAD_DIM:(g + 1) * GROUP * HEAD_DIM] = _unstack_heads(o_a, g, nq).astype(BF16)


def _nsa_swa(pb, pf, kvc, rel_table, sinks):
    b, s, _ = pb.shape
    nq = NSA_Q
    nc = s // NSA_CMP_STRIDE
    n_cmp = (s - NSA_CMP_LEN) // NSA_CMP_STRIDE + 1
    n_sel = s // NSA_SEL_LEN
    per_blk = nq // NSA_CMP_STRIDE
    assert n_sel <= HEAD_DIM and s % FAR_TILE == 0 and 2 * per_blk <= LANES
    h8 = HEADS_PER_MIXER
    f_a = _bias_by_dist(rel_table[:, :h8], NSA_WINDOW + nq)
    f_b = _bias_by_dist(rel_table[:, h8:2 * h8], NSA_WINDOW + nq)

    d_c = (jnp.arange(nq)[:, None] + nq - NSA_CMP_LEN + 1) - NSA_CMP_STRIDE * jnp.arange(2 * per_blk)[None, :]
    tc = jnp.where((d_c >= 0)[None], f_a[:, jnp.maximum(d_c, 0)], NEG)
    tc = jnp.pad(_rows_by_group(tc), ((0, 0), (0, 0), (0, LANES - 2 * per_blk)))
    tc_hi = tc.astype(BF16)
    tc = jnp.stack([tc_hi, (tc - tc_hi.astype(F32)).astype(BF16)])

    tnear = _near_tables(f_a, nq)
    twin = _band_table(f_a, nq, NSA_WINDOW)
    tswa = _band_table(f_b, nq, SWA_WINDOW)

    cmp_start = jnp.arange(nc) * NSA_CMP_STRIDE
    sel_start = jnp.arange(n_sel) * NSA_SEL_LEN
    overlap = jnp.maximum(jnp.minimum(cmp_start[:, None] + NSA_CMP_LEN, sel_start[None, :] + NSA_SEL_LEN)
                          - jnp.maximum(cmp_start[:, None], sel_start[None, :]), 0)
    overlap = jnp.where(jnp.arange(nc)[:, None] < n_cmp, overlap, 0).astype(F32) / NSA_CMP_LEN
    ovw = jnp.zeros((KV_HEADS, nc, LANES), F32)
    ovw = ovw.at[0, :, HEAD_DIM:HEAD_DIM + n_sel].set(overlap).at[1, :, :n_sel].set(overlap)
    ovw = ovw.astype(BF16)

    rows = GROUP * nq
    sink = (sinks.astype(F32) - rel_table[REL_BUCKETS - 1, h8:2 * h8]).reshape(KV_HEADS, GROUP, 1, 1)
    sink = jnp.broadcast_to(sink, (KV_HEADS, GROUP, nq, LANES)).reshape(KV_HEADS, rows, LANES)

    kv = lambda c: pl.BlockSpec((1, s, LANES), lambda i, j: (i, 0, c), pipeline_mode=pl.Buffered(1))
    aug = pltpu.VMEM((KV_HEADS, s, LANES), BF16)
    qsel = pltpu.VMEM((KV_HEADS, rows, LANES), BF16)
    return pl.pallas_call(
        _nsa_swa_kernel,
        grid=(b, s // nq),
        in_specs=[pl.BlockSpec((1, nq, HQ), lambda i, j: (i, j, 0)),
                  pl.BlockSpec((1, nq, HQ), lambda i, j: (i, j, 1)),
                  kv(8), kv(9), kv(10), kv(11), kv(12), kv(13),
                  pl.BlockSpec((1, nq, LANES), lambda i, j: (i, j, 2)),
                  pl.BlockSpec((1, 1, nc, LANES), lambda i, j: (i, 0, 0, 0)),
                  pl.BlockSpec((1, 1, nc, LANES), lambda i, j: (i, 1, 0, 0)),
                  _whole(tc), _whole(tnear), _whole(twin), _whole(tswa), _whole(ovw), _whole(sink)],
        out_specs=pl.BlockSpec((1, nq, 2 * HQ), lambda i, j: (i, j, 0)),
        out_shape=jax.ShapeDtypeStruct((b, s, 2 * HQ), BF16),
        scratch_shapes=[aug, aug,
                        pltpu.VMEM((s + NSA_WINDOW, LANES), BF16),
                        pltpu.VMEM((KV_HEADS, s + NSA_WINDOW, LANES), BF16),
                        pltpu.VMEM((s + SWA_WINDOW, LANES), BF16),
                        pltpu.VMEM((KV_HEADS, s + SWA_WINDOW, LANES), BF16),
                        qsel, qsel,
                        pltpu.VMEM((KV_HEADS, rows, LANES), F32),
                        pltpu.VMEM((KV_HEADS, rows, LANES), F32), pltpu.VMEM((KV_HEADS, rows, LANES), F32)],
        compiler_params=_cparams(("arbitrary", "arbitrary")),
        name="nsa_swa",
    )(pb, pb, pb, pb, pb, pb, pb, pb, pf, kvc, kvc, tc, tnear, twin, tswa, ovw, sink)


def _moba_kernel(q_ref, k_ref, v_ref, tnear_ref, o_ref, kaug_ref, vaug_ref, kmx_ref,
                 q_far_ref, q_near_ref, m_ref, acc_ref):
    qi = pl.program_id(1)
    nq = MOBA_BLOCK
    rows = GROUP * nq
    nb = k_ref.shape[1] // nq

    @pl.when(qi == 0)
    def _():
        _fill_aug(k_ref, kaug_ref, MOBA_BLOCK)
        _fill_aug(v_ref, vaug_ref, 0)
        kmx_ref[...] = jnp.zeros_like(kmx_ref)
        lane = lax.broadcasted_iota(jnp.int32, (1, LANES), 1)
        for blk in range(nb):
            mean = jnp.mean(k_ref[0, blk * nq:(blk + 1) * nq, :].astype(F32), axis=0, keepdims=True)
            for g in range(KV_HEADS):
                hi, lo = _split2(jnp.where((lane // HEAD_DIM) == g, mean, 0.0))
                r = HEAD_DIM * (1 - g) + blk
                kmx_ref[0, g, r:r + 1, :] = hi
                kmx_ref[1, g, r:r + 1, :] = lo

    lane = lax.broadcasted_iota(jnp.int32, (rows, LANES), 1)
    works, qzs = [], []
    for g in range(KV_HEADS):
        qz = _stack_heads(q_ref, g, nq)
        qz_b16 = qz.astype(BF16)
        gs = _dot_nt(qz_b16, kmx_ref[0, g]) + _dot_nt(qz_b16, kmx_ref[1, g])
        j = lane - HEAD_DIM * (1 - g)
        works.append(jnp.where((j >= 0) & (j < qi), gs, BELOW_NEG))
        qzs.append(qz)
    _, picked = _pick_top(jnp.concatenate(works, axis=0), MOBA_TOP)
    for g in range(KV_HEADS):
        pk = picked[g * rows:(g + 1) * rows]
        j = lane - HEAD_DIM * (1 - g)
        q_far_ref[g] = _with_aux(qzs[g], jnp.where(pk & (j < qi - 1), 0.0, MASK), g)
        q_near_ref[g] = _with_aux(qzs[g], jnp.where((pk & (j >= qi - 1)) | (j == qi), 0.0, MASK), g)

    o = _sweep(q_far_ref, q_near_ref, kaug_ref, vaug_ref, tnear_ref, m_ref, acc_ref, qi, nq)
    for g in range(KV_HEADS):
        o_ref[0, :, g * GROUP * HEAD_DIM:(g + 1) * GROUP * HEAD_DIM] = _unstack_heads(o[g], g, nq).astype(BF16)


def _moba(pb, rel_table, q_col, k_col, v_col):
    b, s, _ = pb.shape
    nq = MOBA_BLOCK
    nb = s // nq
    assert s % FAR_TILE == 0 and nb >= 2 and nb <= HEAD_DIM
    tnear = _near_tables(_bias_by_dist(rel_table[:, :HEADS_PER_MIXER], 2 * nq), nq)
    rows = GROUP * nq
    aug = pltpu.VMEM((KV_HEADS, s, LANES), BF16)
    qsel = pltpu.VMEM((KV_HEADS, rows, LANES), BF16)
    kv = lambda c: pl.BlockSpec((1, s, LANES), lambda i, j: (i, 0, c), pipeline_mode=pl.Buffered(1))
    return pl.pallas_call(
        _moba_kernel,
        grid=(b, nb),
        in_specs=[pl.BlockSpec((1, nq, HQ), lambda i, j: (i, j, q_col)), kv(k_col), kv(v_col), _whole(tnear)],
        out_specs=pl.BlockSpec((1, nq, HQ), lambda i, j: (i, j, 0)),
        out_shape=jax.ShapeDtypeStruct((b, s, HQ), BF16),
        scratch_shapes=[aug, aug,
                        pltpu.VMEM((2, KV_HEADS, LANES, LANES), BF16),
                        qsel, qsel,
                        pltpu.VMEM((KV_HEADS, rows, LANES), F32), pltpu.VMEM((KV_HEADS, rows, LANES), F32)],
        compiler_params=_cparams(("arbitrary", "arbitrary")),
        name="moba",
    )(pb, pb, pb, tnear)


def _stick_kernel(q_ref, k_ref, v_ref, tri_ref, o_ref, acc_ref, c_ref):
    qi = pl.program_id(1)
    t_len = SB_T
    pairs = HEADS_PER_MIXER // 2
    lane = lax.broadcasted_iota(jnp.int32, (t_len, LANES), 1)
    strict = lax.broadcasted_iota(jnp.int32, (t_len, t_len), 1) < lax.broadcasted_iota(jnp.int32, (t_len, t_len), 0)

    def tile(t, diag):
        rows = pl.ds(pl.multiple_of(t * t_len, t_len), t_len)
        heads = range(2 * pairs)
        zs = []
        for pair in range(pairs):
            cols = slice(pair * LANES, (pair + 1) * LANES)
            q = q_ref[0, :, cols].astype(F32) * SCALE
            for half in range(2):
                zs.append(_dot_nt(jnp.where((lane // HEAD_DIM) == half, q, 0.0).astype(BF16),
                                  k_ref[0, rows, cols]))
        logs = [-(jnp.maximum(z, 0.0) + jnp.log(1.0 + jnp.exp(-jnp.abs(z)))) for z in zs]
        if diag:
            logs = [jnp.where(strict, log_1m, 0.0) for log_1m in logs]
        local = [_dot(jnp.concatenate(_split2(log_1m), axis=1), tri_ref[...]) for log_1m in logs]
        for h in heads:
            total = jnp.broadcast_to(local[h][:, :1] + logs[h][:, :1], (t_len, LANES))
            c = jnp.zeros((t_len, LANES), F32) if diag else c_ref[h]
            after = local[h] + jnp.concatenate([c] * (t_len // LANES), axis=1)
            w = jnp.exp(logs[h] + zs[h] + after)
            v = v_ref[0, rows, (h // 2) * LANES:(h // 2 + 1) * LANES]
            if diag:
                acc_ref[h] = _dot(jnp.where(strict, w, 0.0).astype(BF16), v)
            else:
                acc_ref[h] += _dot(w.astype(BF16), v)
            c_ref[h] = c + total

    tile(qi, True)

    def cond(carry):
        t, c_max = carry
        return (t >= 0) & (c_max >= EXP_ZERO)

    def body(carry):
        t, _ = carry
        tile(t, False)
        return t - 1, jnp.max(c_ref[...])

    lax.while_loop(cond, body, (qi - 1, jnp.max(c_ref[...])))
    for pair in range(pairs):
        o_ref[0, :, pair * LANES:(pair + 1) * LANES] = jnp.where(
            (lane // HEAD_DIM) == 0, acc_ref[2 * pair], acc_ref[2 * pair + 1]).astype(BF16)


def _stick_breaking(pb, q_col, k_col, v_col):
    b, s, _ = pb.shape
    t_len = SB_T
    assert s % t_len == 0
    tri = (jnp.arange(t_len)[:, None] > jnp.arange(t_len)[None, :]).astype(BF16)
    tri = jnp.concatenate([tri, tri], axis=0)
    return pl.pallas_call(
        _stick_kernel,
        grid=(b, s // t_len),
        in_specs=[pl.BlockSpec((1, t_len, HQ), lambda i, j: (i, j, q_col)),
                  pl.BlockSpec((1, s, HQ), lambda i, j: (i, 0, k_col)),
                  pl.BlockSpec((1, s, HQ), lambda i, j: (i, 0, v_col)),
                  _whole(tri)],
        out_specs=pl.BlockSpec((1, t_len, HQ), lambda i, j: (i, j, 0)),
        out_shape=jax.ShapeDtypeStruct((b, s, HQ), BF16),
        scratch_shapes=[pltpu.VMEM((HEADS_PER_MIXER, t_len, LANES), F32),
                        pltpu.VMEM((HEADS_PER_MIXER, t_len, LANES), F32)],
        compiler_params=_cparams(("arbitrary", "arbitrary")),
        name="stick_breaking",
    )(pb, pb, pb, tri)


def _split_cols(w, sizes):
    offs = [0]
    for z in sizes:
        offs.append(offs[-1] + z)
    return [w[:, offs[i]:offs[i + 1]] for i in range(len(sizes))]


def kernel(x, c, rel_table, mod_w, mod_b, norm_w, w_in_ab, w_out_ab, nsa_cmp_wk, nsa_cmp_wv,
           nsa_cmp_pe, swa_sinks, w_in_cd, w_out_cd, ffn_w_in, ffn_w_out):
    depth = mod_w.shape[0]
    d = x.shape[-1]
    h8 = HEADS_PER_MIXER
    mod = _modulation(c, mod_w.reshape(depth * 2, d, 3 * d), mod_b.reshape(depth * 2, 3 * d))
    mod = mod.reshape(depth, 2, c.shape[0], 3, d)

    for layer in range(depth):
        shift, scale, gate = mod[layer, 0, :, 0], mod[layer, 0, :, 1], mod[layer, 0, :, 2]
        i = layer // 2
        if layer % 2 == 0:
            qa, kca, vca, ksa, vsa, kwa, vwa, ga, qb, kb, vb = _split_cols(
                w_in_ab[i], [HQ, KVW, KVW, KVW, KVW, KVW, KVW, 3 * h8, HQ, KVW, KVW])
            ga = jnp.pad(ga, ((0, 0), (0, LANES - 3 * h8)))
            w = jnp.concatenate([qa, qb, ksa, vsa, kwa, vwa, kb, vb, kca, vca, ga], axis=1).astype(BF16)
            nb = 2 * HQ + 6 * KVW
            pb, pf = _in_proj(x, norm_w[layer, 0, 0], scale, shift, w, nb, 3 * LANES, LANES)
            kvc = _compress(pf, nsa_cmp_wk[i], nsa_cmp_wv[i], nsa_cmp_pe[i])
            o = _nsa_swa(pb, pf, kvc, rel_table, swa_sinks[i])
            mixed, w_mix = (o, 0, o, 1), w_out_ab[i]
        else:
            qc, kc, vc, qd, kd, vd = _split_cols(w_in_cd[i], [HQ, KVW, KVW, HQ, HQ, HQ])
            w = jnp.concatenate([qc, qd, kd, vd, kc, vc], axis=1).astype(BF16)
            (pb,) = _in_proj(x, norm_w[layer, 0, 0], scale, shift, w, 4 * HQ + 2 * KVW, 0, 0)
            o_c = _moba(pb, rel_table, 0, 4 * HQ // LANES, 4 * HQ // LANES + 1)
            o_d = _stick_breaking(pb, 1, 2, 3)
            mixed, w_mix = (o_c, 0, o_d, 0), w_out_cd[i]

        shift, scale, gate_ffn = mod[layer, 1, :, 0], mod[layer, 1, :, 1], mod[layer, 1, :, 2]
        x = _mix_ffn(*mixed, w_mix.astype(BF16), x, norm_w[layer, 0, 1], gate,
                     norm_w[layer, 1, 0], scale, shift, ffn_w_in[layer].astype(BF16),
                     ffn_w_out[layer].astype(BF16), norm_w[layer, 1, 1], gate_ffn)
    return x
```

```python
import functools
import math

import jax
import jax.numpy as jnp
from jax import lax
from jax.experimental import pallas as pl
from jax.experimental.pallas import tpu as pltpu

F32 = jnp.float32
BF16 = jnp.bfloat16

HEAD_DIM = 64
LANES = 128
KV_HEADS = 2
GROUP = 4
HEADS_PER_MIXER = KV_HEADS * GROUP
HQ = HEADS_PER_MIXER * HEAD_DIM
KVW = KV_HEADS * HEAD_DIM
NSA_Q = 256
NSA_CMP_LEN = 32
NSA_CMP_STRIDE = 16
NSA_SEL_LEN = 64
NSA_SEL_TOP = 8
NSA_WINDOW = 512
NSA_FORCE = 1e4
SWA_WINDOW = 128
MOBA_BLOCK = 256
MOBA_TOP = 3
REL_BUCKETS = 32
REL_MAX_DIST = 128
RMS_EPS = 1e-6
NEG = -1e30
MASK = -(2.0 ** 100)
BELOW_NEG = -3e38
SCALE = HEAD_DIM ** -0.5
FAR_TILE = 512
SB_T = 256
EXP_ZERO = -104.0
VMEM_LIMIT = 60 * 1024 * 1024

_NT = (((1,), (1,)), ((), ()))


def _cparams(sem):
    return pltpu.CompilerParams(dimension_semantics=sem, vmem_limit_bytes=VMEM_LIMIT)


def _dot(a, b):
    return jnp.dot(a, b, preferred_element_type=F32)


def _dot_nt(a, b):
    return lax.dot_general(a, b, _NT, preferred_element_type=F32)


def _split2(x):
    hi = x.astype(BF16)
    lo = (x - hi.astype(F32)).astype(BF16)
    return hi, lo


def _rms(x, w):
    return x * lax.rsqrt(jnp.mean(x * x, axis=-1, keepdims=True) + RMS_EPS) * w


def _whole(a):
    return pl.BlockSpec(a.shape, lambda *_: (0,) * a.ndim)


def _mod_kernel(c_ref, w_ref, b_ref, o_ref):
    o_ref[0] = jnp.dot(c_ref[...], w_ref[0], preferred_element_type=F32,
                       precision=lax.Precision.HIGHEST) + b_ref[0]


def _modulation(c, mod_w, mod_b):
    n, d, d3 = mod_w.shape
    b = c.shape[0]
    tn = 1024
    return pl.pallas_call(
        _mod_kernel,
        grid=(n, d3 // tn),
        in_specs=[pl.BlockSpec((b, d), lambda i, j: (0, 0)),
                  pl.BlockSpec((1, d, tn), lambda i, j: (i, 0, j)),
                  pl.BlockSpec((1, 1, tn), lambda i, j: (i, 0, j))],
        out_specs=pl.BlockSpec((1, b, tn), lambda i, j: (i, 0, j)),
        out_shape=jax.ShapeDtypeStruct((n, b, d3), F32),
        compiler_params=_cparams(("arbitrary", "arbitrary")),
        name="modulation",
    )(c, mod_w, mod_b.reshape(n, 1, d3))


def _in_proj_kernel(x_ref, nw_ref, sc_ref, sh_ref, w_ref, *out_refs, nb, nf, gate_cols):
    x = x_ref[0]
    h = _rms(x, nw_ref[...]) * (1.0 + sc_ref[0]) + sh_ref[0]
    hb = h.astype(BF16)
    ob_ref = out_refs[0]
    for c0 in range(0, nb, 512):
        c1 = min(c0 + 512, nb)
        ob_ref[0, :, c0:c1] = _dot(hb, w_ref[:, c0:c1]).astype(BF16)
    if nf:
        of_ref = out_refs[1]
        y = _dot(hb, w_ref[:, nb:nb + nf])
        if gate_cols:
            y_g = jax.nn.sigmoid(y[:, nf - gate_cols:])
            of_ref[0, :, :nf - gate_cols] = y[:, :nf - gate_cols]
            of_ref[0, :, nf - gate_cols:] = y_g
        else:
            of_ref[0] = y


def _in_proj(x, norm_w, scale, shift, w, nb, nf, gate_cols, tm=512):
    b, s, d = x.shape
    tm = min(tm, s)
    out_shape = [jax.ShapeDtypeStruct((b, s, nb), BF16)]
    out_specs = [pl.BlockSpec((1, tm, nb), lambda i, j: (i, j, 0))]
    if nf:
        out_shape.append(jax.ShapeDtypeStruct((b, s, nf), F32))
        out_specs.append(pl.BlockSpec((1, tm, nf), lambda i, j: (i, j, 0)))
    return pl.pallas_call(
        functools.partial(_in_proj_kernel, nb=nb, nf=nf, gate_cols=gate_cols),
        grid=(b, s // tm),
        in_specs=[pl.BlockSpec((1, tm, d), lambda i, j: (i, j, 0)),
                  pl.BlockSpec((1, d), lambda i, j: (0, 0)),
                  pl.BlockSpec((1, 1, d), lambda i, j: (i, 0, 0)),
                  pl.BlockSpec((1, 1, d), lambda i, j: (i, 0, 0)),
                  pl.BlockSpec((d, nb + nf), lambda i, j: (0, 0))],
        out_specs=out_specs,
        out_shape=out_shape,
        compiler_params=_cparams(("arbitrary", "arbitrary")),
        name="in_proj",
    )(x, norm_w.reshape(1, d), scale.reshape(b, 1, d), shift.reshape(b, 1, d), w)


def _mix_ffn_kernel(o1_ref, o2_ref, wm1_ref, wm2_ref, x_ref, nwm_ref, gm_ref,
                    nw1_ref, sc_ref, sh_ref, wi_ref, wo_ref, nw2_ref, gate_ref, xo_ref, *, tf):
    dff = wo_ref.shape[0]
    y = _dot(o1_ref[0], wm1_ref[...]) + _dot(o2_ref[0], wm2_ref[...])
    x = x_ref[0] + gm_ref[0] * _rms(y, nwm_ref[...])
    hb = (_rms(x, nw1_ref[...]) * (1.0 + sc_ref[0]) + sh_ref[0]).astype(BF16)
    y = None
    for c in range(0, dff, tf):
        g = _dot(hb, wi_ref[:, c:c + tf])
        u = _dot(hb, wi_ref[:, dff + c:dff + c + tf])
        a = (g * jax.nn.sigmoid(g) * u).astype(BF16)
        part = _dot(a, wo_ref[c:c + tf, :])
        y = part if y is None else y + part
    xo_ref[0] = x + gate_ref[0] * _rms(y, nw2_ref[...])


def _mix_ffn(o1, c1, o2, c2, w_mix, x, nw_mix, gate_mix, nw1, scale, shift, w_in, w_out, nw2, gate,
             tm=512, tf=1408):
    b, s, d = x.shape
    dff = w_out.shape[0]
    tm = min(tm, s)
    assert dff % tf == 0 and tf % LANES == 0
    vec = lambda a: a.reshape(b, 1, d)
    row = pl.BlockSpec((1, d), lambda i, j: (0, 0))
    per_batch = pl.BlockSpec((1, 1, d), lambda i, j: (i, 0, 0))
    return pl.pallas_call(
        functools.partial(_mix_ffn_kernel, tf=tf),
        grid=(b, s // tm),
        in_specs=[pl.BlockSpec((1, tm, HQ), lambda i, j: (i, j, c1)),
                  pl.BlockSpec((1, tm, HQ), lambda i, j: (i, j, c2)),
                  pl.BlockSpec((HQ, d), lambda i, j: (0, 0)),
                  pl.BlockSpec((HQ, d), lambda i, j: (1, 0)),
                  pl.BlockSpec((1, tm, d), lambda i, j: (i, j, 0)),
                  row, per_batch, row, per_batch, per_batch,
                  _whole(w_in), _whole(w_out), row, per_batch],
        out_specs=pl.BlockSpec((1, tm, d), lambda i, j: (i, j, 0)),
        out_shape=jax.ShapeDtypeStruct((b, s, d), F32),
        compiler_params=_cparams(("arbitrary", "arbitrary")),
        name="mix_ffn",
    )(o1, o2, w_mix, w_mix, x, nw_mix.reshape(1, d), vec(gate_mix),
      nw1.reshape(1, d), vec(scale), vec(shift), w_in, w_out, nw2.reshape(1, d), vec(gate))


def _rel_bucket(dist):
    n = jnp.maximum(dist, 0)
    max_exact = REL_BUCKETS // 2
    nf = jnp.maximum(n, 1).astype(jnp.float32)
    large = max_exact + (jnp.log(nf / max_exact) / math.log(REL_MAX_DIST / max_exact)
                         * (REL_BUCKETS - max_exact)).astype(jnp.int32)
    large = jnp.minimum(large, REL_BUCKETS - 1)
    return jnp.where(n < max_exact, n, large)


def _bias_by_dist(tab, n):
    return (tab[_rel_bucket(jnp.arange(n))] - tab[REL_BUCKETS - 1]).T.astype(F32)


def _toeplitz(f, nq, nk, offset):
    p = nq + nk - 1
    k = jnp.arange(p)
    delta = jnp.where(k < nk, k, k - p)
    w = f[:, jnp.clip(offset - delta, 0, f.shape[1] - 1)]
    flat = jnp.tile(w, (1, nq))[:, :nq * (p - 1)]
    return flat.reshape(f.shape[0], nq, p - 1)[:, :, :nk]


def _rows_by_group(t):
    h, nq, nk = t.shape
    return t.reshape(KV_HEADS, GROUP * nq, nk)


def _dist(nq, nk, offset):
    return (jnp.arange(nq)[:, None] + offset) - jnp.arange(nk)[None, :]


def _masked(t, valid):
    return jnp.where(valid[None], t, NEG)


def _band_table(f, nq, window):
    d = _dist(nq, window + nq, window)
    return _rows_by_group(_masked(_toeplitz(f, nq, window + nq, window), (d >= 0) & (d < window)))


def _near_tables(f, nq):
    d_first = _dist(nq, 2 * nq, 0)
    d_near = _dist(nq, 2 * nq, nq)
    return jnp.stack([_rows_by_group(_masked(_toeplitz(f, nq, 2 * nq, 0), d_first >= 0)),
                      _rows_by_group(_masked(_toeplitz(f, nq, 2 * nq, nq), d_near >= 0))])


def _compress_kernel(k_ref, w_ref, pe_ref, o_ref):
    nc = o_ref.shape[2]
    half = NSA_CMP_LEN // 2
    lo = jnp.zeros((nc, LANES), F32)
    hi = jnp.zeros((nc, LANES), F32)
    for l in range(half):
        rows = k_ref[0, pl.ds(l, nc, stride=NSA_CMP_STRIDE), :]
        lo = lo + _dot((rows + pe_ref[l:l + 1, :]).astype(BF16), w_ref[0, l])
        hi = hi + _dot((rows + pe_ref[half + l:half + l + 1, :]).astype(BF16), w_ref[0, half + l])
    o_ref[0, 0] = (lo + pltpu.roll(hi, nc - 1, axis=0)).astype(BF16)


def _compress(f32_proj, wk, wv, pe):
    b, s, _ = f32_proj.shape
    nc = s // NSA_CMP_STRIDE
    eye = jnp.eye(KV_HEADS, dtype=F32)
    bd = lambda w: jnp.einsum('gh,lde->lgdhe', eye, w).reshape(NSA_CMP_LEN, LANES, LANES)
    w = jnp.stack([bd(wk), bd(wv)]).astype(BF16)
    pe2 = jnp.tile(pe, (1, KV_HEADS)).astype(F32)
    return pl.pallas_call(
        _compress_kernel,
        grid=(b, 2),
        in_specs=[pl.BlockSpec((1, s, LANES), lambda i, j: (i, 0, j)),
                  pl.BlockSpec((1, NSA_CMP_LEN, LANES, LANES), lambda i, j: (j, 0, 0, 0)),
                  pl.BlockSpec((NSA_CMP_LEN, LANES), lambda i, j: (0, 0))],
        out_specs=pl.BlockSpec((1, 1, nc, LANES), lambda i, j: (i, j, 0, 0)),
        out_shape=jax.ShapeDtypeStruct((b, 2, nc, LANES), BF16),
        compiler_params=_cparams(("arbitrary", "arbitrary")),
        name="nsa_compress",
    )(f32_proj, w, pe2)


def _rowmax(s):
    mx = s[:, :LANES]
    for c in range(LANES, s.shape[1], LANES):
        mx = jnp.maximum(mx, s[:, c:c + LANES])
    return jnp.broadcast_to(jnp.max(mx, axis=1, keepdims=True), (s.shape[0], LANES))


def _exp_sub(s, m):
    return jnp.concatenate([jnp.exp(s[:, c:c + LANES] - m).astype(BF16)
                            for c in range(0, s.shape[1], LANES)], axis=1)


def _flash_step(scores, v_augs, m_ref, acc_ref):
    groups = range(len(scores))
    m_prev = [m_ref[g] for g in groups]
    m_new = [jnp.maximum(m_prev[g], _rowmax(scores[g])) for g in groups]
    p = [_exp_sub(scores[g], m_new[g]) for g in groups]
    for g in groups:
        acc_ref[g] = jnp.exp(m_prev[g] - m_new[g]) * acc_ref[g] + _dot(p[g], v_augs[g])
        m_ref[g] = m_new[g]


def _normalize(acc):
    return acc / pltpu.roll(acc, HEAD_DIM, axis=1)


def _head_slab(q_ref, h, g):
    slab = q_ref[0, :, LANES * (h // 2):LANES * (h // 2) + LANES].astype(F32) * SCALE
    if (h % 2) != g:
        slab = pltpu.roll(slab, HEAD_DIM, axis=1)
    return slab


def _stack_heads(q_ref, g, nq):
    lane = lax.broadcasted_iota(jnp.int32, (nq, LANES), 1)
    in_data = (lane // HEAD_DIM) == g
    return jnp.concatenate(
        [jnp.where(in_data, _head_slab(q_ref, GROUP * g + r, g), 0.0) for r in range(GROUP)], axis=0)


def _unstack_heads(o, g, nq):
    lane = lax.broadcasted_iota(jnp.int32, (nq, LANES), 1)
    slabs = []
    for m in range(GROUP // 2):
        even = o[(2 * m) * nq:(2 * m + 1) * nq]
        odd = o[(2 * m + 1) * nq:(2 * m + 2) * nq]
        if g == 0:
            odd = pltpu.roll(odd, HEAD_DIM, axis=1)
        else:
            even = pltpu.roll(even, HEAD_DIM, axis=1)
        slabs.append(jnp.where(lane < HEAD_DIM, even, odd))
    return jnp.concatenate(slabs, axis=1)


def _fill_aug(src_ref, dst_ref, block_len, row0=0):
    s = src_ref.shape[1]
    ch = min(512, s)
    for g in range(KV_HEADS):
        for c in range(s // ch):
            lane = lax.broadcasted_iota(jnp.int32, (ch, LANES), 1)
            in_data = (lane // HEAD_DIM) == g
            if block_len:
                key = lax.broadcasted_iota(jnp.int32, (ch, LANES), 0) + c * ch
                aux = (lane == (HEAD_DIM * (1 - g) + key // block_len)).astype(BF16)
            else:
                aux = jnp.ones((ch, LANES), BF16)
            dst_ref[g, row0 + c * ch:row0 + (c + 1) * ch, :] = jnp.where(
                in_data, src_ref[0, c * ch:(c + 1) * ch, :], aux)


def _pick_top(work, count):
    cand_f = lax.broadcasted_iota(jnp.int32, work.shape, 0).astype(F32)
    picked = jnp.zeros(work.shape, F32)
    picked_real = jnp.zeros(work.shape, F32)
    for _ in range(count):
        mx = jnp.max(work, axis=0, keepdims=True)
        idx = jnp.min(jnp.where(work == mx, cand_f, float(LANES)), axis=0, keepdims=True)
        pick = cand_f == idx
        picked = jnp.where(pick, 1.0, picked)
        picked_real = jnp.where(pick & (mx > 0.5 * NEG), 1.0, picked_real)
        work = jnp.where(pick, BELOW_NEG, work)
    return picked.T > 0.5, picked_real.T > 0.5


def _with_aux(qz, aux, g):
    lane = lax.broadcasted_iota(jnp.int32, qz.shape, 1)
    return jnp.where((lane // HEAD_DIM) == g, qz, aux).astype(BF16)


def _band_scores(qz, k_pad_ref, t_ref, g, qi, nq, window):
    span = pl.ds(pl.multiple_of(qi * nq, LANES), window + nq)
    s = _dot_nt(qz, k_pad_ref[span, :]) + t_ref[g]
    col = lax.broadcasted_iota(jnp.int32, s.shape, 1)
    return jnp.where(col < window - qi * nq, NEG, s), span


def _sweep(q_far_ref, q_near_ref, kaug_ref, vaug_ref, tnear_ref, m_ref, acc_ref, qi, nq):
    groups = range(KV_HEADS)
    m_ref[...] = jnp.full_like(m_ref, NEG)
    acc_ref[...] = jnp.zeros_like(acc_ref)

    def far_step(t, carry):
        span = pl.ds(pl.multiple_of(t * FAR_TILE, FAR_TILE), FAR_TILE)
        _flash_step([_dot_nt(q_far_ref[g], kaug_ref[g, span, :]) for g in groups],
                    [vaug_ref[g, span, :] for g in groups], m_ref, acc_ref)
        return carry

    far_keys = jnp.maximum(qi - 1, 0) * nq
    lax.fori_loop(0, (far_keys + FAR_TILE - 1) // FAR_TILE, far_step, 0)
    span = pl.ds(pl.multiple_of(jnp.maximum(qi - 1, 0) * nq, nq), 2 * nq)
    first = jnp.minimum(qi, 1)
    _flash_step([_dot_nt(q_near_ref[g], kaug_ref[g, span, :]) + tnear_ref[first, g] for g in groups],
                [vaug_ref[g, span, :] for g in groups], m_ref, acc_ref)
    return [_normalize(acc_ref[g]) for g in groups]


def _nsa_swa_kernel(qa_ref, qb_ref, ksel_ref, vsel_ref, kwin_ref, vwin_ref, kb_ref, vb_ref,
                    gates_ref, kc_ref, vc_ref, tc_ref, tnear_ref, twin_ref, tswa_ref,
                    ovw_ref, sink_ref, o_ref,
                    kaug_ref, vsel_aug_ref, kwin_pad_ref, vwin_aug_ref, kb_pad_ref, vb_aug_ref,
                    q_far_ref, q_near_ref, part_ref, m_ref, acc_ref):
    qi = pl.program_id(1)
    nq = NSA_Q
    rows = GROUP * nq
    nc = kc_ref.shape[2]

    @pl.when(qi == 0)
    def _():
        _fill_aug(ksel_ref, kaug_ref, NSA_SEL_LEN)
        _fill_aug(vsel_ref, vsel_aug_ref, 0)
        kwin_pad_ref[:NSA_WINDOW, :] = jnp.zeros((NSA_WINDOW, LANES), BF16)
        kwin_pad_ref[NSA_WINDOW:, :] = kwin_ref[0]
        vwin_aug_ref[:, :NSA_WINDOW, :] = jnp.zeros((KV_HEADS, NSA_WINDOW, LANES), BF16)
        _fill_aug(vwin_ref, vwin_aug_ref, 0, row0=NSA_WINDOW)
        kb_pad_ref[:SWA_WINDOW, :] = jnp.zeros((SWA_WINDOW, LANES), BF16)
        kb_pad_ref[SWA_WINDOW:, :] = kb_ref[0]
        vb_aug_ref[:, :SWA_WINDOW, :] = jnp.zeros((KV_HEADS, SWA_WINDOW, LANES), BF16)
        _fill_aug(vb_ref, vb_aug_ref, 0, row0=SWA_WINDOW)

    lane = lax.broadcasted_iota(jnp.int32, (nq, LANES), 1)
    ql = lax.broadcasted_iota(jnp.int32, (nq, LANES), 0)
    gates = gates_ref[0]

    def gate_col(g, branch):
        cols = [jnp.broadcast_to(gates[:, 3 * (GROUP * g + r) + branch:3 * (GROUP * g + r) + branch + 1],
                                 (nq, LANES)) for r in range(GROUP)]
        return jnp.concatenate(cols, axis=0)

    per_blk = nq // NSA_CMP_STRIDE
    u_id = lax.broadcasted_iota(jnp.int32, (LANES, nc), 0)
    n_id = lax.broadcasted_iota(jnp.int32, (LANES, nc), 1)
    shift = ((u_id < 2 * per_blk) & (n_id == per_blk * (qi - 1) + u_id)).astype(BF16)
    cmp_future = lax.broadcasted_iota(jnp.int32, (rows, nc), 1) >= per_blk * (qi + 1)

    groups = range(KV_HEADS)
    qzs = [_stack_heads(qa_ref, g, nq) for g in groups]
    qz_b16 = [qz.astype(BF16) for qz in qzs]

    s = [_dot_nt(qz_b16[g], kc_ref[0, 0]) + _dot(tc_ref[0, g], shift) + _dot(tc_ref[1, g], shift)
         for g in groups]
    s = [jnp.where(cmp_future, NEG, s[g]) for g in groups]
    e = [jnp.exp(s[g] - jnp.max(s[g], axis=1, keepdims=True)) for g in groups]
    p = [jnp.where(s[g] > 0.5 * NEG, e[g], 0.0) / jnp.sum(e[g], axis=1, keepdims=True) for g in groups]
    o_cmp = [_dot(p[g].astype(BF16), vc_ref[0, 0]) for g in groups]

    works = []
    cand = lax.broadcasted_iota(jnp.int32, (LANES, nq), 0)
    own = (nq // NSA_SEL_LEN) * qi + lax.broadcasted_iota(jnp.int32, (LANES, nq), 1) // NSA_SEL_LEN
    for g in groups:
        p4 = p[g][0:nq] + p[g][nq:2 * nq] + p[g][2 * nq:3 * nq] + p[g][3 * nq:4 * nq]
        p_hi, p_lo = _split2(p4)
        imp = _dot_nt(ovw_ref[g], p_hi) + _dot_nt(ovw_ref[g], p_lo)
        j = cand - HEAD_DIM * (1 - g)
        forced = (j == 0) | (j == own) | (j == own - 1)
        imp = jnp.where(j <= own, imp + jnp.where(forced, NSA_FORCE, 0.0), NEG)
        works.append(jnp.where((cand // HEAD_DIM) == (1 - g), imp, BELOW_NEG))

    band = [_band_scores(qz_b16[g], kwin_pad_ref, twin_ref, g, qi, nq, NSA_WINDOW) for g in groups]
    pw = [_exp_sub(s_w, _rowmax(s_w)) for s_w, _ in band]
    for g in groups:
        o_win = _normalize(_dot(pw[g], vwin_aug_ref[g, band[g][1], :]))
        part_ref[g] = gate_col(g, 0) * o_cmp[g] + gate_col(g, 2) * o_win

    qz_b = [_stack_heads(qb_ref, g, nq).astype(BF16) for g in groups]
    band = [_band_scores(qz_b[g], kb_pad_ref, tswa_ref, g, qi, nq, SWA_WINDOW) for g in groups]
    m_b = [jnp.maximum(_rowmax(band[g][0]), sink_ref[g]) for g in groups]
    pb = [_exp_sub(band[g][0], m_b[g]) for g in groups]
    for g in groups:
        r = _dot(pb[g], vb_aug_ref[g, band[g][1], :])
        o_b = r / (pltpu.roll(r, HEAD_DIM, axis=1) + jnp.exp(sink_ref[g] - m_b[g]))
        o_ref[0, :, HQ + g * GROUP * HEAD_DIM:HQ + (g + 1) * GROUP * HEAD_DIM] = (
            _unstack_heads(o_b, g, nq).astype(BF16))

    picked, _ = _pick_top(jnp.concatenate(works, axis=1), NSA_SEL_TOP)
    far_blocks = (nq // NSA_SEL_LEN) * (qi - 1)
    for g in range(KV_HEADS):
        pk = picked[g * nq:(g + 1) * nq]
        j = lane - HEAD_DIM * (1 - g)
        sel_far = jnp.where(pk & (j < far_blocks), 0.0, MASK)
        sel_near = jnp.where(pk & (j >= far_blocks), 0.0, MASK)
        q_far_ref[g] = _with_aux(qzs[g], jnp.concatenate([sel_far] * GROUP, axis=0), g)
        q_near_ref[g] = _with_aux(qzs[g], jnp.concatenate([sel_near] * GROUP, axis=0), g)

    o_sel = _sweep(q_far_ref, q_near_ref, kaug_ref, vsel_aug_ref, tnear_ref, m_ref, acc_ref, qi, nq)
    for g in groups:
        o_a = part_ref[g] + gate_col(g, 1) * o_sel[g]
        o_ref[0, :, g * GROUP * HEAD_DIM:(g + 1) * GROUP * HEAD_DIM] = _unstack_heads(o_a, g, nq).astype(BF16)


def _nsa_swa(pb, pf, kvc, rel_table, sinks):
    b, s, _ = pb.shape
    nq = NSA_Q
    nc = s // NSA_CMP_STRIDE
    n_cmp = (s - NSA_CMP_LEN) // NSA_CMP_STRIDE + 1
    n_sel = s // NSA_SEL_LEN
    per_blk = nq // NSA_CMP_STRIDE
    assert n_sel <= HEAD_DIM and s % FAR_TILE == 0 and 2 * per_blk <= LANES
    h8 = HEADS_PER_MIXER
    f_a = _bias_by_dist(rel_table[:, :h8], NSA_WINDOW + nq)
    f_b = _bias_by_dist(rel_table[:, h8:2 * h8], NSA_WINDOW + nq)

    d_c = (jnp.arange(nq)[:, None] + nq - NSA_CMP_LEN + 1) - NSA_CMP_STRIDE * jnp.arange(2 * per_blk)[None, :]
    tc = jnp.where((d_c >= 0)[None], f_a[:, jnp.maximum(d_c, 0)], NEG)
    tc = jnp.pad(_rows_by_group(tc), ((0, 0), (0, 0), (0, LANES - 2 * per_blk)))
    tc_hi = tc.astype(BF16)
    tc = jnp.stack([tc_hi, (tc - tc_hi.astype(F32)).astype(BF16)])

    tnear = _near_tables(f_a, nq)
    twin = _band_table(f_a, nq, NSA_WINDOW)
    tswa = _band_table(f_b, nq, SWA_WINDOW)

    cmp_start = jnp.arange(nc) * NSA_CMP_STRIDE
    sel_start = jnp.arange(n_sel) * NSA_SEL_LEN
    overlap = jnp.maximum(jnp.minimum(cmp_start[:, None] + NSA_CMP_LEN, sel_start[None, :] + NSA_SEL_LEN)
                          - jnp.maximum(cmp_start[:, None], sel_start[None, :]), 0)
    overlap = jnp.where(jnp.arange(nc)[:, None] < n_cmp, overlap, 0).astype(F32) / NSA_CMP_LEN
    ovw = jnp.zeros((KV_HEADS, nc, LANES), F32)
    ovw = ovw.at[0, :, HEAD_DIM:HEAD_DIM + n_sel].set(overlap).at[1, :, :n_sel].set(overlap)
    ovw = ovw.transpose(0, 2, 1).astype(BF16)

    rows = GROUP * nq
    sink = (sinks.astype(F32) - rel_table[REL_BUCKETS - 1, h8:2 * h8]).reshape(KV_HEADS, GROUP, 1, 1)
    sink = jnp.broadcast_to(sink, (KV_HEADS, GROUP, nq, LANES)).reshape(KV_HEADS, rows, LANES)

    kv = lambda c: pl.BlockSpec((1, s, LANES), lambda i, j: (i, 0, c), pipeline_mode=pl.Buffered(1))
    aug = pltpu.VMEM((KV_HEADS, s, LANES), BF16)
    qsel = pltpu.VMEM((KV_HEADS, rows, LANES), BF16)
    return pl.pallas_call(
        _nsa_swa_kernel,
        grid=(b, s // nq),
        in_specs=[pl.BlockSpec((1, nq, HQ), lambda i, j: (i, j, 0)),
                  pl.BlockSpec((1, nq, HQ), lambda i, j: (i, j, 1)),
                  kv(8), kv(9), kv(10), kv(11), kv(12), kv(13),
                  pl.BlockSpec((1, nq, LANES), lambda i, j: (i, j, 2)),
                  pl.BlockSpec((1, 1, nc, LANES), lambda i, j: (i, 0, 0, 0)),
                  pl.BlockSpec((1, 1, nc, LANES), lambda i, j: (i, 1, 0, 0)),
                  _whole(tc), _whole(tnear), _whole(twin), _whole(tswa), _whole(ovw), _whole(sink)],
        out_specs=pl.BlockSpec((1, nq, 2 * HQ), lambda i, j: (i, j, 0)),
        out_shape=jax.ShapeDtypeStruct((b, s, 2 * HQ), BF16),
        scratch_shapes=[aug, aug,
                        pltpu.VMEM((s + NSA_WINDOW, LANES), BF16),
                        pltpu.VMEM((KV_HEADS, s + NSA_WINDOW, LANES), BF16),
                        pltpu.VMEM((s + SWA_WINDOW, LANES), BF16),
                        pltpu.VMEM((KV_HEADS, s + SWA_WINDOW, LANES), BF16),
                        qsel, qsel,
                        pltpu.VMEM((KV_HEADS, rows, LANES), F32),
                        pltpu.VMEM((KV_HEADS, rows, LANES), F32), pltpu.VMEM((KV_HEADS, rows, LANES), F32)],
        compiler_params=_cparams(("arbitrary", "arbitrary")),
        name="nsa_swa",
    )(pb, pb, pb, pb, pb, pb, pb, pb, pf, kvc, kvc, tc, tnear, twin, tswa, ovw, sink)


def _moba_kernel(q_ref, k_ref, v_ref, tnear_ref, o_ref, kaug_ref, vaug_ref, kmx_ref,
                 q_far_ref, q_near_ref, m_ref, acc_ref):
    qi = pl.program_id(1)
    nq = MOBA_BLOCK
    rows = GROUP * nq
    nb = k_ref.shape[1] // nq

    @pl.when(qi == 0)
    def _():
        _fill_aug(k_ref, kaug_ref, MOBA_BLOCK)
        _fill_aug(v_ref, vaug_ref, 0)
        kmx_ref[...] = jnp.zeros_like(kmx_ref)
        lane = lax.broadcasted_iota(jnp.int32, (1, LANES), 1)
        for blk in range(nb):
            mean = jnp.mean(k_ref[0, blk * nq:(blk + 1) * nq, :].astype(F32), axis=0, keepdims=True)
            for g in range(KV_HEADS):
                hi, lo = _split2(jnp.where((lane // HEAD_DIM) == g, mean, 0.0))
                r = HEAD_DIM * (1 - g) + blk
                kmx_ref[0, g, r:r + 1, :] = hi
                kmx_ref[1, g, r:r + 1, :] = lo

    lane = lax.broadcasted_iota(jnp.int32, (rows, LANES), 1)
    works, qzs = [], []
    cand = lax.broadcasted_iota(jnp.int32, (LANES, rows), 0)
    for g in range(KV_HEADS):
        qz = _stack_heads(q_ref, g, nq)
        qz_b16 = qz.astype(BF16)
        gs = _dot_nt(kmx_ref[0, g], qz_b16) + _dot_nt(kmx_ref[1, g], qz_b16)
        j = cand - HEAD_DIM * (1 - g)
        works.append(jnp.where((j >= 0) & (j < qi), gs, BELOW_NEG))
        qzs.append(qz)
    _, picked = _pick_top(jnp.concatenate(works, axis=1), MOBA_TOP)
    for g in range(KV_HEADS):
        pk = picked[g * rows:(g + 1) * rows]
        j = lane - HEAD_DIM * (1 - g)
        q_far_ref[g] = _with_aux(qzs[g], jnp.where(pk & (j < qi - 1), 0.0, MASK), g)
        q_near_ref[g] = _with_aux(qzs[g], jnp.where((pk & (j >= qi - 1)) | (j == qi), 0.0, MASK), g)

    o = _sweep(q_far_ref, q_near_ref, kaug_ref, vaug_ref, tnear_ref, m_ref, acc_ref, qi, nq)
    for g in range(KV_HEADS):
        o_ref[0, :, g * GROUP * HEAD_DIM:(g + 1) * GROUP * HEAD_DIM] = _unstack_heads(o[g], g, nq).astype(BF16)


def _moba(pb, rel_table, q_col, k_col, v_col):
    b, s, _ = pb.shape
    nq = MOBA_BLOCK
    nb = s // nq
    assert s % FAR_TILE == 0 and nb >= 2 and nb <= HEAD_DIM
    tnear = _near_tables(_bias_by_dist(rel_table[:, :HEADS_PER_MIXER], 2 * nq), nq)
    rows = GROUP * nq
    aug = pltpu.VMEM((KV_HEADS, s, LANES), BF16)
    qsel = pltpu.VMEM((KV_HEADS, rows, LANES), BF16)
    kv = lambda c: pl.BlockSpec((1, s, LANES), lambda i, j: (i, 0, c), pipeline_mode=pl.Buffered(1))
    return pl.pallas_call(
        _moba_kernel,
        grid=(b, nb),
        in_specs=[pl.BlockSpec((1, nq, HQ), lambda i, j: (i, j, q_col)), kv(k_col), kv(v_col), _whole(tnear)],
        out_specs=pl.BlockSpec((1, nq, HQ), lambda i, j: (i, j, 0)),
        out_shape=jax.ShapeDtypeStruct((b, s, HQ), BF16),
        scratch_shapes=[aug, aug,
                        pltpu.VMEM((2, KV_HEADS, LANES, LANES), BF16),
                        qsel, qsel,
                        pltpu.VMEM((KV_HEADS, rows, LANES), F32), pltpu.VMEM((KV_HEADS, rows, LANES), F32)],
        compiler_params=_cparams(("arbitrary", "arbitrary")),
        name="moba",
    )(pb, pb, pb, tnear)


def _stick_kernel(q_ref, k_ref, v_ref, tri_ref, o_ref, acc_ref, c_ref):
    qi = pl.program_id(1)
    t_len = SB_T
    pairs = HEADS_PER_MIXER // 2
    lane = lax.broadcasted_iota(jnp.int32, (t_len, LANES), 1)
    strict = lax.broadcasted_iota(jnp.int32, (t_len, t_len), 1) < lax.broadcasted_iota(jnp.int32, (t_len, t_len), 0)

    def tile(t, diag):
        rows = pl.ds(pl.multiple_of(t * t_len, t_len), t_len)
        heads = range(2 * pairs)
        zs = []
        for pair in range(pairs):
            cols = slice(pair * LANES, (pair + 1) * LANES)
            q = q_ref[0, :, cols].astype(F32) * SCALE
            for half in range(2):
                zs.append(_dot_nt(jnp.where((lane // HEAD_DIM) == half, q, 0.0).astype(BF16),
                                  k_ref[0, rows, cols]))
        logs = [-(jnp.maximum(z, 0.0) + jnp.log(1.0 + jnp.exp(-jnp.abs(z)))) for z in zs]
        if diag:
            logs = [jnp.where(strict, log_1m, 0.0) for log_1m in logs]
        local = [_dot(jnp.concatenate(_split2(log_1m), axis=1), tri_ref[...]) for log_1m in logs]
        for h in heads:
            total = jnp.broadcast_to(local[h][:, :1] + logs[h][:, :1], (t_len, LANES))
            c = jnp.zeros((t_len, LANES), F32) if diag else c_ref[h]
            after = local[h] + jnp.concatenate([c] * (t_len // LANES), axis=1)
            w = jnp.exp(logs[h] + zs[h] + after)
            v = v_ref[0, rows, (h // 2) * LANES:(h // 2 + 1) * LANES]
            if diag:
                acc_ref[h] = _dot(jnp.where(strict, w, 0.0).astype(BF16), v)
            else:
                acc_ref[h] += _dot(w.astype(BF16), v)
            c_ref[h] = c + total

    tile(qi, True)

    def cond(carry):
        t, c_max = carry
        return (t >= 0) & (c_max >= EXP_ZERO)

    def body(carry):
        t, _ = carry
        tile(t, False)
        return t - 1, jnp.max(c_ref[...])

    lax.while_loop(cond, body, (qi - 1, jnp.max(c_ref[...])))
    for pair in range(pairs):
        o_ref[0, :, pair * LANES:(pair + 1) * LANES] = jnp.where(
            (lane // HEAD_DIM) == 0, acc_ref[2 * pair], acc_ref[2 * pair + 1]).astype(BF16)


def _stick_breaking(pb, q_col, k_col, v_col):
    b, s, _ = pb.shape
    t_len = SB_T
    assert s % t_len == 0
    tri = (jnp.arange(t_len)[:, None] > jnp.arange(t_len)[None, :]).astype(BF16)
    tri = jnp.concatenate([tri, tri], axis=0)
    return pl.pallas_call(
        _stick_kernel,
        grid=(b, s // t_len),
        in_specs=[pl.BlockSpec((1, t_len, HQ), lambda i, j: (i, j, q_col)),
                  pl.BlockSpec((1, s, HQ), lambda i, j: (i, 0, k_col)),
                  pl.BlockSpec((1, s, HQ), lambda i, j: (i, 0, v_col)),
                  _whole(tri)],
        out_specs=pl.BlockSpec((1, t_len, HQ), lambda i, j: (i, j, 0)),
        out_shape=jax.ShapeDtypeStruct((b, s, HQ), BF16),
        scratch_shapes=[pltpu.VMEM((HEADS_PER_MIXER, t_len, LANES), F32),
                        pltpu.VMEM((HEADS_PER_MIXER, t_len, LANES), F32)],
        compiler_params=_cparams(("arbitrary", "arbitrary")),
        name="stick_breaking",
    )(pb, pb, pb, tri)


def _split_cols(w, sizes):
    offs = [0]
    for z in sizes:
        offs.append(offs[-1] + z)
    return [w[:, offs[i]:offs[i + 1]] for i in range(len(sizes))]


def kernel(x, c, rel_table, mod_w, mod_b, norm_w, w_in_ab, w_out_ab, nsa_cmp_wk, nsa_cmp_wv,
           nsa_cmp_pe, swa_sinks, w_in_cd, w_out_cd, ffn_w_in, ffn_w_out):
    depth = mod_w.shape[0]
    d = x.shape[-1]
    h8 = HEADS_PER_MIXER
    mod = _modulation(c, mod_w.reshape(depth * 2, d, 3 * d), mod_b.reshape(depth * 2, 3 * d))
    mod = mod.reshape(depth, 2, c.shape[0], 3, d)

    for layer in range(depth):
        shift, scale, gate = mod[layer, 0, :, 0], mod[layer, 0, :, 1], mod[layer, 0, :, 2]
        i = layer // 2
        if layer % 2 == 0:
            qa, kca, vca, ksa, vsa, kwa, vwa, ga, qb, kb, vb = _split_cols(
                w_in_ab[i], [HQ, KVW, KVW, KVW, KVW, KVW, KVW, 3 * h8, HQ, KVW, KVW])
            ga = jnp.pad(ga, ((0, 0), (0, LANES - 3 * h8)))
            w = jnp.concatenate([qa, qb, ksa, vsa, kwa, vwa, kb, vb, kca, vca, ga], axis=1).astype(BF16)
            nb = 2 * HQ + 6 * KVW
            pb, pf = _in_proj(x, norm_w[layer, 0, 0], scale, shift, w, nb, 3 * LANES, LANES)
            kvc = _compress(pf, nsa_cmp_wk[i], nsa_cmp_wv[i], nsa_cmp_pe[i])
            o = _nsa_swa(pb, pf, kvc, rel_table, swa_sinks[i])
            mixed, w_mix = (o, 0, o, 1), w_out_ab[i]
        else:
            qc, kc, vc, qd, kd, vd = _split_cols(w_in_cd[i], [HQ, KVW, KVW, HQ, HQ, HQ])
            w = jnp.concatenate([qc, qd, kd, vd, kc, vc], axis=1).astype(BF16)
            (pb,) = _in_proj(x, norm_w[layer, 0, 0], scale, shift, w, 4 * HQ + 2 * KVW, 0, 0)
            o_c = _moba(pb, rel_table, 0, 4 * HQ // LANES, 4 * HQ // LANES + 1)
            o_d = _stick_breaking(pb, 1, 2, 3)
            mixed, w_mix = (o_c, 0, o_d, 0), w_out_cd[i]

        shift, scale, gate_ffn = mod[layer, 1, :, 0], mod[layer, 1, :, 1], mod[layer, 1, :, 2]
        x = _mix_ffn(*mixed, w_mix.astype(BF16), x, norm_w[layer, 0, 1], gate,
                     norm_w[layer, 1, 0], scale, shift, ffn_w_in[layer].astype(BF16),
                     ffn_w_out[layer].astype(BF16), norm_w[layer, 1, 1], gate_ffn)
    return x
```

```python
import functools
import math

import jax
import jax.numpy as jnp
from jax import lax
from jax.experimental import pallas as pl
from jax.experimental.pallas import tpu as pltpu

F32 = jnp.float32
BF16 = jnp.bfloat16

HEAD_DIM = 64
LANES = 128
KV_HEADS = 2
GROUP = 4
HEADS_PER_MIXER = KV_HEADS * GROUP
HQ = HEADS_PER_MIXER * HEAD_DIM
KVW = KV_HEADS * HEAD_DIM
NSA_Q = 256
NSA_CMP_LEN = 32
NSA_CMP_STRIDE = 16
NSA_SEL_LEN = 64
NSA_SEL_TOP = 8
NSA_WINDOW = 512
NSA_FORCE = 1e4
SWA_WINDOW = 128
MOBA_BLOCK = 256
MOBA_TOP = 3
REL_BUCKETS = 32
REL_MAX_DIST = 128
RMS_EPS = 1e-6
NEG = -1e30
MASK = -(2.0 ** 100)
BELOW_NEG = -3e38
SCALE = HEAD_DIM ** -0.5
FAR_TILE = 512
SB_T = 256
EXP_ZERO = -104.0
VMEM_LIMIT = 60 * 1024 * 1024

_NT = (((1,), (1,)), ((), ()))


def _cparams(sem):
    return pltpu.CompilerParams(dimension_semantics=sem, vmem_limit_bytes=VMEM_LIMIT)


def _dot(a, b):
    return jnp.dot(a, b, preferred_element_type=F32)


def _dot_nt(a, b):
    return lax.dot_general(a, b, _NT, preferred_element_type=F32)


def _split2(x):
    hi = x.astype(BF16)
    lo = (x - hi.astype(F32)).astype(BF16)
    return hi, lo


def _rms(x, w):
    return x * lax.rsqrt(jnp.mean(x * x, axis=-1, keepdims=True) + RMS_EPS) * w


def _whole(a):
    return pl.BlockSpec(a.shape, lambda *_: (0,) * a.ndim)


def _mod_kernel(c_ref, w_ref, b_ref, o_ref):
    o_ref[0] = jnp.dot(c_ref[...], w_ref[0], preferred_element_type=F32,
                       precision=lax.Precision.HIGHEST) + b_ref[0]


def _modulation(c, mod_w, mod_b):
    n, d, d3 = mod_w.shape
    b = c.shape[0]
    tn = 1024
    return pl.pallas_call(
        _mod_kernel,
        grid=(n, d3 // tn),
        in_specs=[pl.BlockSpec((b, d), lambda i, j: (0, 0)),
                  pl.BlockSpec((1, d, tn), lambda i, j: (i, 0, j)),
                  pl.BlockSpec((1, 1, tn), lambda i, j: (i, 0, j))],
        out_specs=pl.BlockSpec((1, b, tn), lambda i, j: (i, 0, j)),
        out_shape=jax.ShapeDtypeStruct((n, b, d3), F32),
        compiler_params=_cparams(("arbitrary", "arbitrary")),
        name="modulation",
    )(c, mod_w, mod_b.reshape(n, 1, d3))


def _in_proj_kernel(x_ref, nw_ref, sc_ref, sh_ref, w_ref, *out_refs, nb, nf, gate_cols):
    x = x_ref[0]
    h = _rms(x, nw_ref[...]) * (1.0 + sc_ref[0]) + sh_ref[0]
    hb = h.astype(BF16)
    ob_ref = out_refs[0]
    for c0 in range(0, nb, 512):
        c1 = min(c0 + 512, nb)
        ob_ref[0, :, c0:c1] = _dot(hb, w_ref[:, c0:c1]).astype(BF16)
    if nf:
        of_ref = out_refs[1]
        y = _dot(hb, w_ref[:, nb:nb + nf])
        if gate_cols:
            y_g = jax.nn.sigmoid(y[:, nf - gate_cols:])
            of_ref[0, :, :nf - gate_cols] = y[:, :nf - gate_cols]
            of_ref[0, :, nf - gate_cols:] = y_g
        else:
            of_ref[0] = y


def _in_proj(x, norm_w, scale, shift, w, nb, nf, gate_cols, tm=512):
    b, s, d = x.shape
    tm = min(tm, s)
    out_shape = [jax.ShapeDtypeStruct((b, s, nb), BF16)]
    out_specs = [pl.BlockSpec((1, tm, nb), lambda i, j: (i, j, 0))]
    if nf:
        out_shape.append(jax.ShapeDtypeStruct((b, s, nf), F32))
        out_specs.append(pl.BlockSpec((1, tm, nf), lambda i, j: (i, j, 0)))
    return pl.pallas_call(
        functools.partial(_in_proj_kernel, nb=nb, nf=nf, gate_cols=gate_cols),
        grid=(b, s // tm),
        in_specs=[pl.BlockSpec((1, tm, d), lambda i, j: (i, j, 0)),
                  pl.BlockSpec((1, d), lambda i, j: (0, 0)),
                  pl.BlockSpec((1, 1, d), lambda i, j: (i, 0, 0)),
                  pl.BlockSpec((1, 1, d), lambda i, j: (i, 0, 0)),
                  pl.BlockSpec((d, nb + nf), lambda i, j: (0, 0))],
        out_specs=out_specs,
        out_shape=out_shape,
        compiler_params=_cparams(("arbitrary", "arbitrary")),
        name="in_proj",
    )(x, norm_w.reshape(1, d), scale.reshape(b, 1, d), shift.reshape(b, 1, d), w)


def _mix_ffn_kernel(o1_ref, o2_ref, wm1_ref, wm2_ref, x_ref, nwm_ref, gm_ref,
                    nw1_ref, sc_ref, sh_ref, wi_ref, wo_ref, nw2_ref, gate_ref, xo_ref, *, tf):
    dff = wo_ref.shape[0]
    y = _dot(o1_ref[0], wm1_ref[...]) + _dot(o2_ref[0], wm2_ref[...])
    x = x_ref[0] + gm_ref[0] * _rms(y, nwm_ref[...])
    hb = (_rms(x, nw1_ref[...]) * (1.0 + sc_ref[0]) + sh_ref[0]).astype(BF16)
    y = None
    for c in range(0, dff, tf):
        g = _dot(hb, wi_ref[:, c:c + tf])
        u = _dot(hb, wi_ref[:, dff + c:dff + c + tf])
        a = (g * jax.nn.sigmoid(g) * u).astype(BF16)
        part = _dot(a, wo_ref[c:c + tf, :])
        y = part if y is None else y + part
    xo_ref[0] = x + gate_ref[0] * _rms(y, nw2_ref[...])


def _mix_ffn(o1, c1, o2, c2, w_mix, x, nw_mix, gate_mix, nw1, scale, shift, w_in, w_out, nw2, gate,
             tm=512, tf=1408):
    b, s, d = x.shape
    dff = w_out.shape[0]
    tm = min(tm, s)
    assert dff % tf == 0 and tf % LANES == 0
    vec = lambda a: a.reshape(b, 1, d)
    row = pl.BlockSpec((1, d), lambda i, j: (0, 0))
    per_batch = pl.BlockSpec((1, 1, d), lambda i, j: (i, 0, 0))
    return pl.pallas_call(
        functools.partial(_mix_ffn_kernel, tf=tf),
        grid=(b, s // tm),
        in_specs=[pl.BlockSpec((1, tm, HQ), lambda i, j: (i, j, c1)),
                  pl.BlockSpec((1, tm, HQ), lambda i, j: (i, j, c2)),
                  pl.BlockSpec((HQ, d), lambda i, j: (0, 0)),
                  pl.BlockSpec((HQ, d), lambda i, j: (1, 0)),
                  pl.BlockSpec((1, tm, d), lambda i, j: (i, j, 0)),
                  row, per_batch, row, per_batch, per_batch,
                  _whole(w_in), _whole(w_out), row, per_batch],
        out_specs=pl.BlockSpec((1, tm, d), lambda i, j: (i, j, 0)),
        out_shape=jax.ShapeDtypeStruct((b, s, d), F32),
        compiler_params=_cparams(("arbitrary", "arbitrary")),
        name="mix_ffn",
    )(o1, o2, w_mix, w_mix, x, nw_mix.reshape(1, d), vec(gate_mix),
      nw1.reshape(1, d), vec(scale), vec(shift), w_in, w_out, nw2.reshape(1, d), vec(gate))


def _rel_bucket(dist):
    n = jnp.maximum(dist, 0)
    max_exact = REL_BUCKETS // 2
    nf = jnp.maximum(n, 1).astype(jnp.float32)
    large = max_exact + (jnp.log(nf / max_exact) / math.log(REL_MAX_DIST / max_exact)
                         * (REL_BUCKETS - max_exact)).astype(jnp.int32)
    large = jnp.minimum(large, REL_BUCKETS - 1)
    return jnp.where(n < max_exact, n, large)


def _bias_by_dist(tab, n):
    return (tab[_rel_bucket(jnp.arange(n))] - tab[REL_BUCKETS - 1]).T.astype(F32)


def _toeplitz(f, nq, nk, offset):
    p = nq + nk - 1
    k = jnp.arange(p)
    delta = jnp.where(k < nk, k, k - p)
    w = f[:, jnp.clip(offset - delta, 0, f.shape[1] - 1)]
    flat = jnp.tile(w, (1, nq))[:, :nq * (p - 1)]
    return flat.reshape(f.shape[0], nq, p - 1)[:, :, :nk]


def _rows_by_group(t):
    h, nq, nk = t.shape
    return t.reshape(KV_HEADS, GROUP * nq, nk)


def _dist(nq, nk, offset):
    return (jnp.arange(nq)[:, None] + offset) - jnp.arange(nk)[None, :]


def _masked(t, valid):
    return jnp.where(valid[None], t, NEG)


def _band_table(f, nq, window):
    d = _dist(nq, window + nq, window)
    return _rows_by_group(_masked(_toeplitz(f, nq, window + nq, window), (d >= 0) & (d < window)))


def _near_tables(f, nq):
    d_first = _dist(nq, 2 * nq, 0)
    d_near = _dist(nq, 2 * nq, nq)
    return jnp.stack([_rows_by_group(_masked(_toeplitz(f, nq, 2 * nq, 0), d_first >= 0)),
                      _rows_by_group(_masked(_toeplitz(f, nq, 2 * nq, nq), d_near >= 0))])


def _compress_kernel(k_ref, w_ref, pe_ref, o_ref):
    nc = o_ref.shape[2]
    half = NSA_CMP_LEN // 2
    lo = jnp.zeros((nc, LANES), F32)
    hi = jnp.zeros((nc, LANES), F32)
    for l in range(half):
        rows = k_ref[0, pl.ds(l, nc, stride=NSA_CMP_STRIDE), :]
        lo = lo + _dot((rows + pe_ref[l:l + 1, :]).astype(BF16), w_ref[0, l])
        hi = hi + _dot((rows + pe_ref[half + l:half + l + 1, :]).astype(BF16), w_ref[0, half + l])
    o_ref[0, 0] = (lo + pltpu.roll(hi, nc - 1, axis=0)).astype(BF16)


def _compress(f32_proj, wk, wv, pe):
    b, s, _ = f32_proj.shape
    nc = s // NSA_CMP_STRIDE
    eye = jnp.eye(KV_HEADS, dtype=F32)
    bd = lambda w: jnp.einsum('gh,lde->lgdhe', eye, w).reshape(NSA_CMP_LEN, LANES, LANES)
    w = jnp.stack([bd(wk), bd(wv)]).astype(BF16)
    pe2 = jnp.tile(pe, (1, KV_HEADS)).astype(F32)
    return pl.pallas_call(
        _compress_kernel,
        grid=(b, 2),
        in_specs=[pl.BlockSpec((1, s, LANES), lambda i, j: (i, 0, j)),
                  pl.BlockSpec((1, NSA_CMP_LEN, LANES, LANES), lambda i, j: (j, 0, 0, 0)),
                  pl.BlockSpec((NSA_CMP_LEN, LANES), lambda i, j: (0, 0))],
        out_specs=pl.BlockSpec((1, 1, nc, LANES), lambda i, j: (i, j, 0, 0)),
        out_shape=jax.ShapeDtypeStruct((b, 2, nc, LANES), BF16),
        compiler_params=_cparams(("arbitrary", "arbitrary")),
        name="nsa_compress",
    )(f32_proj, w, pe2)


def _rowmax(s):
    mx = s[:, :LANES]
    for c in range(LANES, s.shape[1], LANES):
        mx = jnp.maximum(mx, s[:, c:c + LANES])
    return jnp.broadcast_to(jnp.max(mx, axis=1, keepdims=True), (s.shape[0], LANES))


def _exp_sub(s, m):
    return jnp.concatenate([jnp.exp(s[:, c:c + LANES] - m).astype(BF16)
                            for c in range(0, s.shape[1], LANES)], axis=1)


def _flash_step(scores, v_augs, m_ref, acc_ref):
    groups = range(len(scores))
    m_prev = [m_ref[g] for g in groups]
    m_new = [jnp.maximum(m_prev[g], _rowmax(scores[g])) for g in groups]
    p = [_exp_sub(scores[g], m_new[g]) for g in groups]
    for g in groups:
        acc_ref[g] = jnp.exp(m_prev[g] - m_new[g]) * acc_ref[g] + _dot(p[g], v_augs[g])
        m_ref[g] = m_new[g]


def _normalize(acc):
    return acc / pltpu.roll(acc, HEAD_DIM, axis=1)


def _head_slab(q_ref, h, g):
    slab = q_ref[0, :, LANES * (h // 2):LANES * (h // 2) + LANES].astype(F32) * SCALE
    if (h % 2) != g:
        slab = pltpu.roll(slab, HEAD_DIM, axis=1)
    return slab


def _stack_heads(q_ref, g, nq):
    lane = lax.broadcasted_iota(jnp.int32, (nq, LANES), 1)
    in_data = (lane // HEAD_DIM) == g
    return jnp.concatenate(
        [jnp.where(in_data, _head_slab(q_ref, GROUP * g + r, g), 0.0) for r in range(GROUP)], axis=0)


def _unstack_heads(o, g, nq):
    lane = lax.broadcasted_iota(jnp.int32, (nq, LANES), 1)
    slabs = []
    for m in range(GROUP // 2):
        even = o[(2 * m) * nq:(2 * m + 1) * nq]
        odd = o[(2 * m + 1) * nq:(2 * m + 2) * nq]
        if g == 0:
            odd = pltpu.roll(odd, HEAD_DIM, axis=1)
        else:
            even = pltpu.roll(even, HEAD_DIM, axis=1)
        slabs.append(jnp.where(lane < HEAD_DIM, even, odd))
    return jnp.concatenate(slabs, axis=1)


def _fill_aug(src_ref, dst_ref, block_len, row0=0):
    s = src_ref.shape[1]
    ch = min(512, s)
    for g in range(KV_HEADS):
        for c in range(s // ch):
            lane = lax.broadcasted_iota(jnp.int32, (ch, LANES), 1)
            in_data = (lane // HEAD_DIM) == g
            if block_len:
                key = lax.broadcasted_iota(jnp.int32, (ch, LANES), 0) + c * ch
                aux = (lane == (HEAD_DIM * (1 - g) + key // block_len)).astype(BF16)
            else:
                aux = jnp.ones((ch, LANES), BF16)
            dst_ref[g, row0 + c * ch:row0 + (c + 1) * ch, :] = jnp.where(
                in_data, src_ref[0, c * ch:(c + 1) * ch, :], aux)


def _pick_top(work, count):
    cand_f = lax.broadcasted_iota(jnp.int32, work.shape, 0).astype(F32)
    picked = jnp.zeros(work.shape, F32)
    picked_real = jnp.zeros(work.shape, F32)
    for _ in range(count):
        mx = jnp.max(work, axis=0, keepdims=True)
        idx = jnp.min(jnp.where(work == mx, cand_f, float(LANES)), axis=0, keepdims=True)
        pick = cand_f == idx
        picked = jnp.where(pick, 1.0, picked)
        picked_real = jnp.where(pick & (mx > 0.5 * NEG), 1.0, picked_real)
        work = jnp.where(pick, BELOW_NEG, work)
    return picked, picked_real


def _with_aux(qz, aux, g):
    lane = lax.broadcasted_iota(jnp.int32, qz.shape, 1)
    return jnp.where((lane // HEAD_DIM) == g, qz, aux).astype(BF16)


def _band_scores(qz, k_pad_ref, t_ref, g, qi, nq, window):
    span = pl.ds(pl.multiple_of(qi * nq, LANES), window + nq)
    s = _dot_nt(qz, k_pad_ref[span, :]) + t_ref[g]
    col = lax.broadcasted_iota(jnp.int32, s.shape, 1)
    return jnp.where(col < window - qi * nq, NEG, s), span


def _sweep(q_far_ref, q_near_ref, kaug_ref, vaug_ref, tnear_ref, m_ref, acc_ref, qi, nq):
    groups = range(KV_HEADS)
    m_ref[...] = jnp.full_like(m_ref, NEG)
    acc_ref[...] = jnp.zeros_like(acc_ref)

    def far_step(t, carry):
        span = pl.ds(pl.multiple_of(t * FAR_TILE, FAR_TILE), FAR_TILE)
        _flash_step([_dot_nt(q_far_ref[g], kaug_ref[g, span, :]) for g in groups],
                    [vaug_ref[g, span, :] for g in groups], m_ref, acc_ref)
        return carry

    far_keys = jnp.maximum(qi - 1, 0) * nq
    lax.fori_loop(0, (far_keys + FAR_TILE - 1) // FAR_TILE, far_step, 0)
    span = pl.ds(pl.multiple_of(jnp.maximum(qi - 1, 0) * nq, nq), 2 * nq)
    first = jnp.minimum(qi, 1)
    _flash_step([_dot_nt(q_near_ref[g], kaug_ref[g, span, :]) + tnear_ref[first, g] for g in groups],
                [vaug_ref[g, span, :] for g in groups], m_ref, acc_ref)
    return [_normalize(acc_ref[g]) for g in groups]


def _nsa_swa_kernel(qa_ref, qb_ref, ksel_ref, vsel_ref, kwin_ref, vwin_ref, kb_ref, vb_ref,
                    gates_ref, kc_ref, vc_ref, tc_ref, tnear_ref, twin_ref, tswa_ref,
                    ovw_ref, sink_ref, o_ref,
                    kaug_ref, vsel_aug_ref, kwin_pad_ref, vwin_aug_ref, kb_pad_ref, vb_aug_ref,
                    q_far_ref, q_near_ref, part_ref, m_ref, acc_ref):
    qi = pl.program_id(1)
    nq = NSA_Q
    rows = GROUP * nq
    nc = kc_ref.shape[2]

    @pl.when(qi == 0)
    def _():
        _fill_aug(ksel_ref, kaug_ref, NSA_SEL_LEN)
        _fill_aug(vsel_ref, vsel_aug_ref, 0)
        kwin_pad_ref[:NSA_WINDOW, :] = jnp.zeros((NSA_WINDOW, LANES), BF16)
        kwin_pad_ref[NSA_WINDOW:, :] = kwin_ref[0]
        vwin_aug_ref[:, :NSA_WINDOW, :] = jnp.zeros((KV_HEADS, NSA_WINDOW, LANES), BF16)
        _fill_aug(vwin_ref, vwin_aug_ref, 0, row0=NSA_WINDOW)
        kb_pad_ref[:SWA_WINDOW, :] = jnp.zeros((SWA_WINDOW, LANES), BF16)
        kb_pad_ref[SWA_WINDOW:, :] = kb_ref[0]
        vb_aug_ref[:, :SWA_WINDOW, :] = jnp.zeros((KV_HEADS, SWA_WINDOW, LANES), BF16)
        _fill_aug(vb_ref, vb_aug_ref, 0, row0=SWA_WINDOW)

    lane = lax.broadcasted_iota(jnp.int32, (nq, LANES), 1)
    ql = lax.broadcasted_iota(jnp.int32, (nq, LANES), 0)
    gates = gates_ref[0]

    def gate_col(g, branch):
        cols = [jnp.broadcast_to(gates[:, 3 * (GROUP * g + r) + branch:3 * (GROUP * g + r) + branch + 1],
                                 (nq, LANES)) for r in range(GROUP)]
        return jnp.concatenate(cols, axis=0)

    per_blk = nq // NSA_CMP_STRIDE
    u_id = lax.broadcasted_iota(jnp.int32, (LANES, nc), 0)
    n_id = lax.broadcasted_iota(jnp.int32, (LANES, nc), 1)
    shift = ((u_id < 2 * per_blk) & (n_id == per_blk * (qi - 1) + u_id)).astype(BF16)
    cmp_future = lax.broadcasted_iota(jnp.int32, (rows, nc), 1) >= per_blk * (qi + 1)

    groups = range(KV_HEADS)
    qzs = [_stack_heads(qa_ref, g, nq) for g in groups]
    qz_b16 = [qz.astype(BF16) for qz in qzs]

    s = [_dot_nt(qz_b16[g], kc_ref[0, 0]) + _dot(tc_ref[0, g], shift) + _dot(tc_ref[1, g], shift)
         for g in groups]
    s = [jnp.where(cmp_future, NEG, s[g]) for g in groups]
    e = [jnp.exp(s[g] - jnp.max(s[g], axis=1, keepdims=True)) for g in groups]
    p = [jnp.where(s[g] > 0.5 * NEG, e[g], 0.0) / jnp.sum(e[g], axis=1, keepdims=True) for g in groups]
    o_cmp = [_dot(p[g].astype(BF16), vc_ref[0, 0]) for g in groups]

    works = []
    cand = lax.broadcasted_iota(jnp.int32, (LANES, nq), 0)
    own = (nq // NSA_SEL_LEN) * qi + lax.broadcasted_iota(jnp.int32, (LANES, nq), 1) // NSA_SEL_LEN
    for g in groups:
        p4 = p[g][0:nq] + p[g][nq:2 * nq] + p[g][2 * nq:3 * nq] + p[g][3 * nq:4 * nq]
        p_hi, p_lo = _split2(p4)
        imp = _dot_nt(ovw_ref[g], p_hi) + _dot_nt(ovw_ref[g], p_lo)
        j = cand - HEAD_DIM * (1 - g)
        forced = (j == 0) | (j == own) | (j == own - 1)
        imp = jnp.where(j <= own, imp + jnp.where(forced, NSA_FORCE, 0.0), NEG)
        works.append(jnp.where((cand // HEAD_DIM) == (1 - g), imp, BELOW_NEG))

    band = [_band_scores(qz_b16[g], kwin_pad_ref, twin_ref, g, qi, nq, NSA_WINDOW) for g in groups]
    pw = [_exp_sub(s_w, _rowmax(s_w)) for s_w, _ in band]
    for g in groups:
        o_win = _normalize(_dot(pw[g], vwin_aug_ref[g, band[g][1], :]))
        part_ref[g] = gate_col(g, 0) * o_cmp[g] + gate_col(g, 2) * o_win

    qz_b = [_stack_heads(qb_ref, g, nq).astype(BF16) for g in groups]
    band = [_band_scores(qz_b[g], kb_pad_ref, tswa_ref, g, qi, nq, SWA_WINDOW) for g in groups]
    m_b = [jnp.maximum(_rowmax(band[g][0]), sink_ref[g]) for g in groups]
    pb = [_exp_sub(band[g][0], m_b[g]) for g in groups]
    for g in groups:
        r = _dot(pb[g], vb_aug_ref[g, band[g][1], :])
        o_b = r / (pltpu.roll(r, HEAD_DIM, axis=1) + jnp.exp(sink_ref[g] - m_b[g]))
        o_ref[0, :, HQ + g * GROUP * HEAD_DIM:HQ + (g + 1) * GROUP * HEAD_DIM] = (
            _unstack_heads(o_b, g, nq).astype(BF16))

    picked = _pick_top(jnp.concatenate(works, axis=1), NSA_SEL_TOP)[0].T > 0.5
    far_blocks = (nq // NSA_SEL_LEN) * (qi - 1)
    for g in range(KV_HEADS):
        pk = picked[g * nq:(g + 1) * nq]
        j = lane - HEAD_DIM * (1 - g)
        sel_far = jnp.where(pk & (j < far_blocks), 0.0, MASK)
        sel_near = jnp.where(pk & (j >= far_blocks), 0.0, MASK)
        q_far_ref[g] = _with_aux(qzs[g], jnp.concatenate([sel_far] * GROUP, axis=0), g)
        q_near_ref[g] = _with_aux(qzs[g], jnp.concatenate([sel_near] * GROUP, axis=0), g)

    o_sel = _sweep(q_far_ref, q_near_ref, kaug_ref, vsel_aug_ref, tnear_ref, m_ref, acc_ref, qi, nq)
    for g in groups:
        o_a = part_ref[g] + gate_col(g, 1) * o_sel[g]
        o_ref[0, :, g * GROUP * HEAD_DIM:(g + 1) * GROUP * HEAD_DIM] = _unstack_heads(o_a, g, nq).astype(BF16)


def _nsa_swa(pb, pf, kvc, rel_table, sinks):
    b, s, _ = pb.shape
    nq = NSA_Q
    nc = s // NSA_CMP_STRIDE
    n_cmp = (s - NSA_CMP_LEN) // NSA_CMP_STRIDE + 1
    n_sel = s // NSA_SEL_LEN
    per_blk = nq // NSA_CMP_STRIDE
    assert n_sel <= HEAD_DIM and s % FAR_TILE == 0 and 2 * per_blk <= LANES
    h8 = HEADS_PER_MIXER
    f_a = _bias_by_dist(rel_table[:, :h8], NSA_WINDOW + nq)
    f_b = _bias_by_dist(rel_table[:, h8:2 * h8], NSA_WINDOW + nq)

    d_c = (jnp.arange(nq)[:, None] + nq - NSA_CMP_LEN + 1) - NSA_CMP_STRIDE * jnp.arange(2 * per_blk)[None, :]
    tc = jnp.where((d_c >= 0)[None], f_a[:, jnp.maximum(d_c, 0)], NEG)
    tc = jnp.pad(_rows_by_group(tc), ((0, 0), (0, 0), (0, LANES - 2 * per_blk)))
    tc_hi = tc.astype(BF16)
    tc = jnp.stack([tc_hi, (tc - tc_hi.astype(F32)).astype(BF16)])

    tnear = _near_tables(f_a, nq)
    twin = _band_table(f_a, nq, NSA_WINDOW)
    tswa = _band_table(f_b, nq, SWA_WINDOW)

    cmp_start = jnp.arange(nc) * NSA_CMP_STRIDE
    sel_start = jnp.arange(n_sel) * NSA_SEL_LEN
    overlap = jnp.maximum(jnp.minimum(cmp_start[:, None] + NSA_CMP_LEN, sel_start[None, :] + NSA_SEL_LEN)
                          - jnp.maximum(cmp_start[:, None], sel_start[None, :]), 0)
    overlap = jnp.where(jnp.arange(nc)[:, None] < n_cmp, overlap, 0).astype(F32) / NSA_CMP_LEN
    ovw = jnp.zeros((KV_HEADS, nc, LANES), F32)
    ovw = ovw.at[0, :, HEAD_DIM:HEAD_DIM + n_sel].set(overlap).at[1, :, :n_sel].set(overlap)
    ovw = ovw.transpose(0, 2, 1).astype(BF16)

    rows = GROUP * nq
    sink = (sinks.astype(F32) - rel_table[REL_BUCKETS - 1, h8:2 * h8]).reshape(KV_HEADS, GROUP, 1, 1)
    sink = jnp.broadcast_to(sink, (KV_HEADS, GROUP, nq, LANES)).reshape(KV_HEADS, rows, LANES)

    kv = lambda c: pl.BlockSpec((1, s, LANES), lambda i, j: (i, 0, c), pipeline_mode=pl.Buffered(1))
    aug = pltpu.VMEM((KV_HEADS, s, LANES), BF16)
    qsel = pltpu.VMEM((KV_HEADS, rows, LANES), BF16)
    return pl.pallas_call(
        _nsa_swa_kernel,
        grid=(b, s // nq),
        in_specs=[pl.BlockSpec((1, nq, HQ), lambda i, j: (i, j, 0)),
                  pl.BlockSpec((1, nq, HQ), lambda i, j: (i, j, 1)),
                  kv(8), kv(9), kv(10), kv(11), kv(12), kv(13),
                  pl.BlockSpec((1, nq, LANES), lambda i, j: (i, j, 2)),
                  pl.BlockSpec((1, 1, nc, LANES), lambda i, j: (i, 0, 0, 0)),
                  pl.BlockSpec((1, 1, nc, LANES), lambda i, j: (i, 1, 0, 0)),
                  _whole(tc), _whole(tnear), _whole(twin), _whole(tswa), _whole(ovw), _whole(sink)],
        out_specs=pl.BlockSpec((1, nq, 2 * HQ), lambda i, j: (i, j, 0)),
        out_shape=jax.ShapeDtypeStruct((b, s, 2 * HQ), BF16),
        scratch_shapes=[aug, aug,
                        pltpu.VMEM((s + NSA_WINDOW, LANES), BF16),
                        pltpu.VMEM((KV_HEADS, s + NSA_WINDOW, LANES), BF16),
                        pltpu.VMEM((s + SWA_WINDOW, LANES), BF16),
                        pltpu.VMEM((KV_HEADS, s + SWA_WINDOW, LANES), BF16),
                        qsel, qsel,
                        pltpu.VMEM((KV_HEADS, rows, LANES), F32),
                        pltpu.VMEM((KV_HEADS, rows, LANES), F32), pltpu.VMEM((KV_HEADS, rows, LANES), F32)],
        compiler_params=_cparams(("arbitrary", "arbitrary")),
        name="nsa_swa",
    )(pb, pb, pb, pb, pb, pb, pb, pb, pf, kvc, kvc, tc, tnear, twin, tswa, ovw, sink)


def _moba_kernel(q_ref, k_ref, v_ref, tnear_ref, o_ref, kaug_ref, vaug_ref, kmx_ref,
                 q_far_ref, q_near_ref, m_ref, acc_ref):
    qi = pl.program_id(1)
    nq = MOBA_BLOCK
    rows = GROUP * nq
    nb = k_ref.shape[1] // nq

    @pl.when(qi == 0)
    def _():
        _fill_aug(k_ref, kaug_ref, MOBA_BLOCK)
        _fill_aug(v_ref, vaug_ref, 0)
        kmx_ref[...] = jnp.zeros_like(kmx_ref)
        lane = lax.broadcasted_iota(jnp.int32, (1, LANES), 1)
        for blk in range(nb):
            mean = jnp.mean(k_ref[0, blk * nq:(blk + 1) * nq, :].astype(F32), axis=0, keepdims=True)
            for g in range(KV_HEADS):
                hi, lo = _split2(jnp.where((lane // HEAD_DIM) == g, mean, 0.0))
                r = HEAD_DIM * (1 - g) + blk
                kmx_ref[0, g, r:r + 1, :] = hi
                kmx_ref[1, g, r:r + 1, :] = lo

    lane = lax.broadcasted_iota(jnp.int32, (rows, LANES), 1)
    works, qzs = [], []
    nb_pad = -(-nb // 8) * 8
    j_blk = lax.broadcasted_iota(jnp.int32, (nb_pad, rows), 0)
    for g in range(KV_HEADS):
        qz = _stack_heads(q_ref, g, nq)
        qz_b16 = qz.astype(BF16)
        lo = HEAD_DIM * (1 - g)
        gs = (_dot_nt(kmx_ref[0, g, lo:lo + nb_pad, :], qz_b16)
              + _dot_nt(kmx_ref[1, g, lo:lo + nb_pad, :], qz_b16))
        works.append(jnp.where(j_blk < qi, gs, BELOW_NEG))
        qzs.append(qz)
    _, picked = _pick_top(jnp.concatenate(works, axis=1), MOBA_TOP)
    for g in range(KV_HEADS):
        lo = HEAD_DIM * (1 - g)
        parts = [jnp.zeros((lo, rows), F32)] * (lo > 0) + [picked[:, g * rows:(g + 1) * rows]]
        parts.append(jnp.zeros((LANES - lo - nb_pad, rows), F32))
        pk = jnp.concatenate(parts, axis=0).T > 0.5
        j = lane - HEAD_DIM * (1 - g)
        q_far_ref[g] = _with_aux(qzs[g], jnp.where(pk & (j < qi - 1), 0.0, MASK), g)
        q_near_ref[g] = _with_aux(qzs[g], jnp.where((pk & (j >= qi - 1)) | (j == qi), 0.0, MASK), g)

    o = _sweep(q_far_ref, q_near_ref, kaug_ref, vaug_ref, tnear_ref, m_ref, acc_ref, qi, nq)
    for g in range(KV_HEADS):
        o_ref[0, :, g * GROUP * HEAD_DIM:(g + 1) * GROUP * HEAD_DIM] = _unstack_heads(o[g], g, nq).astype(BF16)


def _moba(pb, rel_table, q_col, k_col, v_col):
    b, s, _ = pb.shape
    nq = MOBA_BLOCK
    nb = s // nq
    assert s % FAR_TILE == 0 and nb >= 2 and nb <= HEAD_DIM
    tnear = _near_tables(_bias_by_dist(rel_table[:, :HEADS_PER_MIXER], 2 * nq), nq)
    rows = GROUP * nq
    aug = pltpu.VMEM((KV_HEADS, s, LANES), BF16)
    qsel = pltpu.VMEM((KV_HEADS, rows, LANES), BF16)
    kv = lambda c: pl.BlockSpec((1, s, LANES), lambda i, j: (i, 0, c), pipeline_mode=pl.Buffered(1))
    return pl.pallas_call(
        _moba_kernel,
        grid=(b, nb),
        in_specs=[pl.BlockSpec((1, nq, HQ), lambda i, j: (i, j, q_col)), kv(k_col), kv(v_col), _whole(tnear)],
        out_specs=pl.BlockSpec((1, nq, HQ), lambda i, j: (i, j, 0)),
        out_shape=jax.ShapeDtypeStruct((b, s, HQ), BF16),
        scratch_shapes=[aug, aug,
                        pltpu.VMEM((2, KV_HEADS, LANES, LANES), BF16),
                        qsel, qsel,
                        pltpu.VMEM((KV_HEADS, rows, LANES), F32), pltpu.VMEM((KV_HEADS, rows, LANES), F32)],
        compiler_params=_cparams(("arbitrary", "arbitrary")),
        name="moba",
    )(pb, pb, pb, tnear)


def _stick_kernel(q_ref, k_ref, v_ref, tri_ref, o_ref, acc_ref, c_ref):
    qi = pl.program_id(1)
    t_len = SB_T
    pairs = HEADS_PER_MIXER // 2
    lane = lax.broadcasted_iota(jnp.int32, (t_len, LANES), 1)
    strict = lax.broadcasted_iota(jnp.int32, (t_len, t_len), 1) < lax.broadcasted_iota(jnp.int32, (t_len, t_len), 0)

    def tile(t, diag):
        rows = pl.ds(pl.multiple_of(t * t_len, t_len), t_len)
        heads = range(2 * pairs)
        zs = []
        for pair in range(pairs):
            cols = slice(pair * LANES, (pair + 1) * LANES)
            q = q_ref[0, :, cols].astype(F32) * SCALE
            for half in range(2):
                zs.append(_dot_nt(jnp.where((lane // HEAD_DIM) == half, q, 0.0).astype(BF16),
                                  k_ref[0, rows, cols]))
        logs = [-(jnp.maximum(z, 0.0) + jnp.log(1.0 + jnp.exp(-jnp.abs(z)))) for z in zs]
        if diag:
            logs = [jnp.where(strict, log_1m, 0.0) for log_1m in logs]
        local = [_dot(jnp.concatenate(_split2(log_1m), axis=1), tri_ref[...]) for log_1m in logs]
        for h in heads:
            total = jnp.broadcast_to(local[h][:, :1] + logs[h][:, :1], (t_len, LANES))
            c = jnp.zeros((t_len, LANES), F32) if diag else c_ref[h]
            after = local[h] + jnp.concatenate([c] * (t_len // LANES), axis=1)
            w = jnp.exp(logs[h] + zs[h] + after)
            v = v_ref[0, rows, (h // 2) * LANES:(h // 2 + 1) * LANES]
            if diag:
                acc_ref[h] = _dot(jnp.where(strict, w, 0.0).astype(BF16), v)
            else:
                acc_ref[h] += _dot(w.astype(BF16), v)
            c_ref[h] = c + total

    tile(qi, True)

    def cond(carry):
        t, c_max = carry
        return (t >= 0) & (c_max >= EXP_ZERO)

    def body(carry):
        t, _ = carry
        tile(t, False)
        return t - 1, jnp.max(c_ref[...])

    lax.while_loop(cond, body, (qi - 1, jnp.max(c_ref[...])))
    for pair in range(pairs):
        o_ref[0, :, pair * LANES:(pair + 1) * LANES] = jnp.where(
            (lane // HEAD_DIM) == 0, acc_ref[2 * pair], acc_ref[2 * pair + 1]).astype(BF16)


def _stick_breaking(pb, q_col, k_col, v_col):
    b, s, _ = pb.shape
    t_len = SB_T
    assert s % t_len == 0
    tri = (jnp.arange(t_len)[:, None] > jnp.arange(t_len)[None, :]).astype(BF16)
    tri = jnp.concatenate([tri, tri], axis=0)
    return pl.pallas_call(
        _stick_kernel,
        grid=(b, s // t_len),
        in_specs=[pl.BlockSpec((1, t_len, HQ), lambda i, j: (i, j, q_col)),
                  pl.BlockSpec((1, s, HQ), lambda i, j: (i, 0, k_col)),
                  pl.BlockSpec((1, s, HQ), lambda i, j: (i, 0, v_col)),
                  _whole(tri)],
        out_specs=pl.BlockSpec((1, t_len, HQ), lambda i, j: (i, j, 0)),
        out_shape=jax.ShapeDtypeStruct((b, s, HQ), BF16),
        scratch_shapes=[pltpu.VMEM((HEADS_PER_MIXER, t_len, LANES), F32),
                        pltpu.VMEM((HEADS_PER_MIXER, t_len, LANES), F32)],
        compiler_params=_cparams(("arbitrary", "arbitrary")),
        name="stick_breaking",
    )(pb, pb, pb, tri)


def _split_cols(w, sizes):
    offs = [0]
    for z in sizes:
        offs.append(offs[-1] + z)
    return [w[:, offs[i]:offs[i + 1]] for i in range(len(sizes))]


def kernel(x, c, rel_table, mod_w, mod_b, norm_w, w_in_ab, w_out_ab, nsa_cmp_wk, nsa_cmp_wv,
           nsa_cmp_pe, swa_sinks, w_in_cd, w_out_cd, ffn_w_in, ffn_w_out):
    depth = mod_w.shape[0]
    d = x.shape[-1]
    h8 = HEADS_PER_MIXER
    mod = _modulation(c, mod_w.reshape(depth * 2, d, 3 * d), mod_b.reshape(depth * 2, 3 * d))
    mod = mod.reshape(depth, 2, c.shape[0], 3, d)

    for layer in range(depth):
        shift, scale, gate = mod[layer, 0, :, 0], mod[layer, 0, :, 1], mod[layer, 0, :, 2]
        i = layer // 2
        if layer % 2 == 0:
            qa, kca, vca, ksa, vsa, kwa, vwa, ga, qb, kb, vb = _split_cols(
                w_in_ab[i], [HQ, KVW, KVW, KVW, KVW, KVW, KVW, 3 * h8, HQ, KVW, KVW])
            ga = jnp.pad(ga, ((0, 0), (0, LANES - 3 * h8)))
            w = jnp.concatenate([qa, qb, ksa, vsa, kwa, vwa, kb, vb, kca, vca, ga], axis=1).astype(BF16)
            nb = 2 * HQ + 6 * KVW
            pb, pf = _in_proj(x, norm_w[layer, 0, 0], scale, shift, w, nb, 3 * LANES, LANES)
            kvc = _compress(pf, nsa_cmp_wk[i], nsa_cmp_wv[i], nsa_cmp_pe[i])
            o = _nsa_swa(pb, pf, kvc, rel_table, swa_sinks[i])
            mixed, w_mix = (o, 0, o, 1), w_out_ab[i]
        else:
            qc, kc, vc, qd, kd, vd = _split_cols(w_in_cd[i], [HQ, KVW, KVW, HQ, HQ, HQ])
            w = jnp.concatenate([qc, qd, kd, vd, kc, vc], axis=1).astype(BF16)
            (pb,) = _in_proj(x, norm_w[layer, 0, 0], scale, shift, w, 4 * HQ + 2 * KVW, 0, 0)
            o_c = _moba(pb, rel_table, 0, 4 * HQ // LANES, 4 * HQ // LANES + 1)
            o_d = _stick_breaking(pb, 1, 2, 3)
            mixed, w_mix = (o_c, 0, o_d, 0), w_out_cd[i]

        shift, scale, gate_ffn = mod[layer, 1, :, 0], mod[layer, 1, :, 1], mod[layer, 1, :, 2]
        x = _mix_ffn(*mixed, w_mix.astype(BF16), x, norm_w[layer, 0, 1], gate,
                     norm_w[layer, 1, 0], scale, shift, ffn_w_in[layer].astype(BF16),
                     ffn_w_out[layer].astype(BF16), norm_w[layer, 1, 1], gate_ffn)
    return x
```

```python
import functools
import math

import jax
import jax.numpy as jnp
from jax import lax
from jax.experimental import pallas as pl
from jax.experimental.pallas import tpu as pltpu

F32 = jnp.float32
BF16 = jnp.bfloat16

HEAD_DIM = 64
LANES = 128
KV_HEADS = 2
GROUP = 4
HEADS_PER_MIXER = KV_HEADS * GROUP
HQ = HEADS_PER_MIXER * HEAD_DIM
KVW = KV_HEADS * HEAD_DIM
NSA_Q = 256
NSA_CMP_LEN = 32
NSA_CMP_STRIDE = 16
NSA_SEL_LEN = 64
NSA_SEL_TOP = 8
NSA_WINDOW = 512
NSA_FORCE = 1e4
SWA_WINDOW = 128
MOBA_BLOCK = 256
MOBA_TOP = 3
REL_BUCKETS = 32
REL_MAX_DIST = 128
RMS_EPS = 1e-6
NEG = -1e30
MASK = -(2.0 ** 100)
BELOW_NEG = -3e38
SCALE = HEAD_DIM ** -0.5
FAR_TILE = 512
SB_T = 256
EXP_ZERO = -104.0
VMEM_LIMIT = 60 * 1024 * 1024

_NT = (((1,), (1,)), ((), ()))


def _cparams(sem):
    return pltpu.CompilerParams(dimension_semantics=sem, vmem_limit_bytes=VMEM_LIMIT)


def _dot(a, b):
    return jnp.dot(a, b, preferred_element_type=F32)


def _dot_nt(a, b):
    return lax.dot_general(a, b, _NT, preferred_element_type=F32)


def _split2(x):
    hi = x.astype(BF16)
    lo = (x - hi.astype(F32)).astype(BF16)
    return hi, lo


def _rms(x, w):
    return x * lax.rsqrt(jnp.mean(x * x, axis=-1, keepdims=True) + RMS_EPS) * w


def _whole(a):
    return pl.BlockSpec(a.shape, lambda *_: (0,) * a.ndim)


def _mod_kernel(c_ref, w_ref, b_ref, o_ref):
    o_ref[0] = jnp.dot(c_ref[...], w_ref[0], preferred_element_type=F32,
                       precision=lax.Precision.HIGHEST) + b_ref[0]


def _modulation(c, mod_w, mod_b):
    n, d, d3 = mod_w.shape
    b = c.shape[0]
    tn = 1024
    return pl.pallas_call(
        _mod_kernel,
        grid=(n, d3 // tn),
        in_specs=[pl.BlockSpec((b, d), lambda i, j: (0, 0)),
                  pl.BlockSpec((1, d, tn), lambda i, j: (i, 0, j)),
                  pl.BlockSpec((1, 1, tn), lambda i, j: (i, 0, j))],
        out_specs=pl.BlockSpec((1, b, tn), lambda i, j: (i, 0, j)),
        out_shape=jax.ShapeDtypeStruct((n, b, d3), F32),
        compiler_params=_cparams(("arbitrary", "arbitrary")),
        name="modulation",
    )(c, mod_w, mod_b.reshape(n, 1, d3))


def _in_proj_kernel(x_ref, nw_ref, sc_ref, sh_ref, w_ref, *out_refs, nb, nf, gate_cols):
    x = x_ref[0]
    h = _rms(x, nw_ref[...]) * (1.0 + sc_ref[0]) + sh_ref[0]
    hb = h.astype(BF16)
    ob_ref = out_refs[0]
    for c0 in range(0, nb, 512):
        c1 = min(c0 + 512, nb)
        ob_ref[0, :, c0:c1] = _dot(hb, w_ref[:, c0:c1]).astype(BF16)
    if nf:
        of_ref = out_refs[1]
        y = _dot(hb, w_ref[:, nb:nb + nf])
        if gate_cols:
            y_g = jax.nn.sigmoid(y[:, nf - gate_cols:])
            of_ref[0, :, :nf - gate_cols] = y[:, :nf - gate_cols]
            of_ref[0, :, nf - gate_cols:] = y_g
        else:
            of_ref[0] = y


def _in_proj(x, norm_w, scale, shift, w, nb, nf, gate_cols, tm=512):
    b, s, d = x.shape
    tm = min(tm, s)
    out_shape = [jax.ShapeDtypeStruct((b, s, nb), BF16)]
    out_specs = [pl.BlockSpec((1, tm, nb), lambda i, j: (i, j, 0))]
    if nf:
        out_shape.append(jax.ShapeDtypeStruct((b, s, nf), F32))
        out_specs.append(pl.BlockSpec((1, tm, nf), lambda i, j: (i, j, 0)))
    return pl.pallas_call(
        functools.partial(_in_proj_kernel, nb=nb, nf=nf, gate_cols=gate_cols),
        grid=(b, s // tm),
        in_specs=[pl.BlockSpec((1, tm, d), lambda i, j: (i, j, 0)),
                  pl.BlockSpec((1, d), lambda i, j: (0, 0)),
                  pl.BlockSpec((1, 1, d), lambda i, j: (i, 0, 0)),
                  pl.BlockSpec((1, 1, d), lambda i, j: (i, 0, 0)),
                  pl.BlockSpec((d, nb + nf), lambda i, j: (0, 0))],
        out_specs=out_specs,
        out_shape=out_shape,
        compiler_params=_cparams(("arbitrary", "arbitrary")),
        name="in_proj",
    )(x, norm_w.reshape(1, d), scale.reshape(b, 1, d), shift.reshape(b, 1, d), w)


def _mix_ffn_kernel(o1_ref, o2_ref, wm1_ref, wm2_ref, x_ref, nwm_ref, gm_ref,
                    nw1_ref, sc_ref, sh_ref, wi_ref, wo_ref, nw2_ref, gate_ref, xo_ref, *, tf):
    dff = wo_ref.shape[0]
    y = _dot(o1_ref[0], wm1_ref[...]) + _dot(o2_ref[0], wm2_ref[...])
    x = x_ref[0] + gm_ref[0] * _rms(y, nwm_ref[...])
    hb = (_rms(x, nw1_ref[...]) * (1.0 + sc_ref[0]) + sh_ref[0]).astype(BF16)
    y = None
    for c in range(0, dff, tf):
        g = _dot(hb, wi_ref[:, c:c + tf])
        u = _dot(hb, wi_ref[:, dff + c:dff + c + tf])
        a = (g * jax.nn.sigmoid(g) * u).astype(BF16)
        part = _dot(a, wo_ref[c:c + tf, :])
        y = part if y is None else y + part
    xo_ref[0] = x + gate_ref[0] * _rms(y, nw2_ref[...])


def _mix_ffn(o1, c1, o2, c2, w_mix, x, nw_mix, gate_mix, nw1, scale, shift, w_in, w_out, nw2, gate,
             tm=512, tf=1408):
    b, s, d = x.shape
    dff = w_out.shape[0]
    tm = min(tm, s)
    assert dff % tf == 0 and tf % LANES == 0
    vec = lambda a: a.reshape(b, 1, d)
    row = pl.BlockSpec((1, d), lambda i, j: (0, 0))
    per_batch = pl.BlockSpec((1, 1, d), lambda i, j: (i, 0, 0))
    return pl.pallas_call(
        functools.partial(_mix_ffn_kernel, tf=tf),
        grid=(b, s // tm),
        in_specs=[pl.BlockSpec((1, tm, HQ), lambda i, j: (i, j, c1)),
                  pl.BlockSpec((1, tm, HQ), lambda i, j: (i, j, c2)),
                  pl.BlockSpec((HQ, d), lambda i, j: (0, 0)),
                  pl.BlockSpec((HQ, d), lambda i, j: (1, 0)),
                  pl.BlockSpec((1, tm, d), lambda i, j: (i, j, 0)),
                  row, per_batch, row, per_batch, per_batch,
                  _whole(w_in), _whole(w_out), row, per_batch],
        out_specs=pl.BlockSpec((1, tm, d), lambda i, j: (i, j, 0)),
        out_shape=jax.ShapeDtypeStruct((b, s, d), F32),
        compiler_params=_cparams(("arbitrary", "arbitrary")),
        name="mix_ffn",
    )(o1, o2, w_mix, w_mix, x, nw_mix.reshape(1, d), vec(gate_mix),
      nw1.reshape(1, d), vec(scale), vec(shift), w_in, w_out, nw2.reshape(1, d), vec(gate))


def _rel_bucket(dist):
    n = jnp.maximum(dist, 0)
    max_exact = REL_BUCKETS // 2
    nf = jnp.maximum(n, 1).astype(jnp.float32)
    large = max_exact + (jnp.log(nf / max_exact) / math.log(REL_MAX_DIST / max_exact)
                         * (REL_BUCKETS - max_exact)).astype(jnp.int32)
    large = jnp.minimum(large, REL_BUCKETS - 1)
    return jnp.where(n < max_exact, n, large)


def _bias_by_dist(tab, n):
    return (tab[_rel_bucket(jnp.arange(n))] - tab[REL_BUCKETS - 1]).T.astype(F32)


def _toeplitz(f, nq, nk, offset):
    p = nq + nk - 1
    k = jnp.arange(p)
    delta = jnp.where(k < nk, k, k - p)
    w = f[:, jnp.clip(offset - delta, 0, f.shape[1] - 1)]
    flat = jnp.tile(w, (1, nq))[:, :nq * (p - 1)]
    return flat.reshape(f.shape[0], nq, p - 1)[:, :, :nk]


def _rows_by_group(t):
    h, nq, nk = t.shape
    return t.reshape(KV_HEADS, GROUP * nq, nk)


def _dist(nq, nk, offset):
    return (jnp.arange(nq)[:, None] + offset) - jnp.arange(nk)[None, :]


def _masked(t, valid):
    return jnp.where(valid[None], t, NEG)


def _band_table(f, nq, window):
    d = _dist(nq, window + nq, window)
    return _rows_by_group(_masked(_toeplitz(f, nq, window + nq, window), (d >= 0) & (d < window)))


def _near_tables(f, nq):
    d_first = _dist(nq, 2 * nq, 0)
    d_near = _dist(nq, 2 * nq, nq)
    return jnp.stack([_rows_by_group(_masked(_toeplitz(f, nq, 2 * nq, 0), d_first >= 0)),
                      _rows_by_group(_masked(_toeplitz(f, nq, 2 * nq, nq), d_near >= 0))])


def _compress_kernel(k_ref, w_ref, pe_ref, o_ref):
    nc = o_ref.shape[2]
    half = NSA_CMP_LEN // 2
    lo = jnp.zeros((nc, LANES), F32)
    hi = jnp.zeros((nc, LANES), F32)
    for l in range(half):
        rows = k_ref[0, pl.ds(l, nc, stride=NSA_CMP_STRIDE), :]
        lo = lo + _dot((rows + pe_ref[l:l + 1, :]).astype(BF16), w_ref[0, l])
        hi = hi + _dot((rows + pe_ref[half + l:half + l + 1, :]).astype(BF16), w_ref[0, half + l])
    o_ref[0, 0] = (lo + pltpu.roll(hi, nc - 1, axis=0)).astype(BF16)


def _compress(f32_proj, wk, wv, pe):
    b, s, _ = f32_proj.shape
    nc = s // NSA_CMP_STRIDE
    eye = jnp.eye(KV_HEADS, dtype=F32)
    bd = lambda w: jnp.einsum('gh,lde->lgdhe', eye, w).reshape(NSA_CMP_LEN, LANES, LANES)
    w = jnp.stack([bd(wk), bd(wv)]).astype(BF16)
    pe2 = jnp.tile(pe, (1, KV_HEADS)).astype(F32)
    return pl.pallas_call(
        _compress_kernel,
        grid=(b, 2),
        in_specs=[pl.BlockSpec((1, s, LANES), lambda i, j: (i, 0, j)),
                  pl.BlockSpec((1, NSA_CMP_LEN, LANES, LANES), lambda i, j: (j, 0, 0, 0)),
                  pl.BlockSpec((NSA_CMP_LEN, LANES), lambda i, j: (0, 0))],
        out_specs=pl.BlockSpec((1, 1, nc, LANES), lambda i, j: (i, j, 0, 0)),
        out_shape=jax.ShapeDtypeStruct((b, 2, nc, LANES), BF16),
        compiler_params=_cparams(("arbitrary", "arbitrary")),
        name="nsa_compress",
    )(f32_proj, w, pe2)


def _rowmax(s):
    mx = s[:, :LANES]
    for c in range(LANES, s.shape[1], LANES):
        mx = jnp.maximum(mx, s[:, c:c + LANES])
    return jnp.broadcast_to(jnp.max(mx, axis=1, keepdims=True), (s.shape[0], LANES))


def _exp_sub(s, m):
    return jnp.concatenate([jnp.exp(s[:, c:c + LANES] - m).astype(BF16)
                            for c in range(0, s.shape[1], LANES)], axis=1)


def _flash_step(scores, v_augs, m_ref, acc_ref):
    groups = range(len(scores))
    m_prev = [m_ref[g] for g in groups]
    m_new = [jnp.maximum(m_prev[g], _rowmax(scores[g])) for g in groups]
    p = [_exp_sub(scores[g], m_new[g]) for g in groups]
    for g in groups:
        acc_ref[g] = jnp.exp(m_prev[g] - m_new[g]) * acc_ref[g] + _dot(p[g], v_augs[g])
        m_ref[g] = m_new[g]


def _normalize(acc):
    return acc / pltpu.roll(acc, HEAD_DIM, axis=1)


def _head_slab(q_ref, h, g):
    slab = q_ref[0, :, LANES * (h // 2):LANES * (h // 2) + LANES].astype(F32) * SCALE
    if (h % 2) != g:
        slab = pltpu.roll(slab, HEAD_DIM, axis=1)
    return slab


def _stack_heads(q_ref, g, nq):
    lane = lax.broadcasted_iota(jnp.int32, (nq, LANES), 1)
    in_data = (lane // HEAD_DIM) == g
    return jnp.concatenate(
        [jnp.where(in_data, _head_slab(q_ref, GROUP * g + r, g), 0.0) for r in range(GROUP)], axis=0)


def _unstack_heads(o, g, nq):
    lane = lax.broadcasted_iota(jnp.int32, (nq, LANES), 1)
    slabs = []
    for m in range(GROUP // 2):
        even = o[(2 * m) * nq:(2 * m + 1) * nq]
        odd = o[(2 * m + 1) * nq:(2 * m + 2) * nq]
        if g == 0:
            odd = pltpu.roll(odd, HEAD_DIM, axis=1)
        else:
            even = pltpu.roll(even, HEAD_DIM, axis=1)
        slabs.append(jnp.where(lane < HEAD_DIM, even, odd))
    return jnp.concatenate(slabs, axis=1)


def _fill_aug(src_ref, dst_ref, block_len, row0=0):
    s = src_ref.shape[1]
    ch = min(512, s)
    for g in range(KV_HEADS):
        for c in range(s // ch):
            lane = lax.broadcasted_iota(jnp.int32, (ch, LANES), 1)
            in_data = (lane // HEAD_DIM) == g
            if block_len:
                key = lax.broadcasted_iota(jnp.int32, (ch, LANES), 0) + c * ch
                aux = (lane == (HEAD_DIM * (1 - g) + key // block_len)).astype(BF16)
            else:
                aux = jnp.ones((ch, LANES), BF16)
            dst_ref[g, row0 + c * ch:row0 + (c + 1) * ch, :] = jnp.where(
                in_data, src_ref[0, c * ch:(c + 1) * ch, :], aux)


def _pick_top(work, count):
    cand_f = lax.broadcasted_iota(jnp.int32, work.shape, 0).astype(F32)
    picked = jnp.zeros(work.shape, F32)
    picked_real = jnp.zeros(work.shape, F32)
    for _ in range(count):
        mx = jnp.max(work, axis=0, keepdims=True)
        idx = jnp.min(jnp.where(work == mx, cand_f, float(LANES)), axis=0, keepdims=True)
        pick = cand_f == idx
        picked = jnp.where(pick, 1.0, picked)
        picked_real = jnp.where(pick & (mx > 0.5 * NEG), 1.0, picked_real)
        work = jnp.where(pick, BELOW_NEG, work)
    return picked, picked_real


def _with_aux(qz, aux, g):
    lane = lax.broadcasted_iota(jnp.int32, qz.shape, 1)
    return jnp.where((lane // HEAD_DIM) == g, qz, aux).astype(BF16)


def _band_scores(qz, k_pad_ref, t_ref, g, qi, nq, window):
    span = pl.ds(pl.multiple_of(qi * nq, LANES), window + nq)
    s = _dot_nt(qz, k_pad_ref[span, :]) + t_ref[g]
    col = lax.broadcasted_iota(jnp.int32, (s.shape[0], window), 1)
    head = jnp.where(col < window - qi * nq, NEG, s[:, :window])
    return jnp.concatenate([head, s[:, window:]], axis=1), span


def _sweep(q_far_ref, q_near_ref, kaug_ref, vaug_ref, tnear_ref, m_ref, acc_ref, qi, nq):
    groups = range(KV_HEADS)
    m_ref[...] = jnp.full_like(m_ref, NEG)
    acc_ref[...] = jnp.zeros_like(acc_ref)

    def far_step(t, carry):
        span = pl.ds(pl.multiple_of(t * FAR_TILE, FAR_TILE), FAR_TILE)
        _flash_step([_dot_nt(q_far_ref[g], kaug_ref[g, span, :]) for g in groups],
                    [vaug_ref[g, span, :] for g in groups], m_ref, acc_ref)
        return carry

    far_keys = jnp.maximum(qi - 1, 0) * nq
    lax.fori_loop(0, (far_keys + FAR_TILE - 1) // FAR_TILE, far_step, 0)
    span = pl.ds(pl.multiple_of(jnp.maximum(qi - 1, 0) * nq, nq), 2 * nq)
    first = jnp.minimum(qi, 1)
    _flash_step([_dot_nt(q_near_ref[g], kaug_ref[g, span, :]) + tnear_ref[first, g] for g in groups],
                [vaug_ref[g, span, :] for g in groups], m_ref, acc_ref)
    return [_normalize(acc_ref[g]) for g in groups]


def _nsa_swa_kernel(qa_ref, qb_ref, ksel_ref, vsel_ref, kwin_ref, vwin_ref, kb_ref, vb_ref,
                    gates_ref, kc_ref, vc_ref, tc_ref, tnear_ref, twin_ref, tswa_ref,
                    ovw_ref, sink_ref, o_ref,
                    kaug_ref, vsel_aug_ref, kwin_pad_ref, vwin_aug_ref, kb_pad_ref, vb_aug_ref,
                    q_far_ref, q_near_ref, part_ref, m_ref, acc_ref):
    qi = pl.program_id(1)
    nq = NSA_Q
    rows = GROUP * nq
    nc = kc_ref.shape[2]

    @pl.when(qi == 0)
    def _():
        _fill_aug(ksel_ref, kaug_ref, NSA_SEL_LEN)
        _fill_aug(vsel_ref, vsel_aug_ref, 0)
        kwin_pad_ref[:NSA_WINDOW, :] = jnp.zeros((NSA_WINDOW, LANES), BF16)
        kwin_pad_ref[NSA_WINDOW:, :] = kwin_ref[0]
        vwin_aug_ref[:, :NSA_WINDOW, :] = jnp.zeros((KV_HEADS, NSA_WINDOW, LANES), BF16)
        _fill_aug(vwin_ref, vwin_aug_ref, 0, row0=NSA_WINDOW)
        kb_pad_ref[:SWA_WINDOW, :] = jnp.zeros((SWA_WINDOW, LANES), BF16)
        kb_pad_ref[SWA_WINDOW:, :] = kb_ref[0]
        vb_aug_ref[:, :SWA_WINDOW, :] = jnp.zeros((KV_HEADS, SWA_WINDOW, LANES), BF16)
        _fill_aug(vb_ref, vb_aug_ref, 0, row0=SWA_WINDOW)

    lane = lax.broadcasted_iota(jnp.int32, (nq, LANES), 1)
    ql = lax.broadcasted_iota(jnp.int32, (nq, LANES), 0)
    gates = gates_ref[0]

    def gate_col(g, branch):
        cols = [jnp.broadcast_to(gates[:, 3 * (GROUP * g + r) + branch:3 * (GROUP * g + r) + branch + 1],
                                 (nq, LANES)) for r in range(GROUP)]
        return jnp.concatenate(cols, axis=0)

    per_blk = nq // NSA_CMP_STRIDE
    u_id = lax.broadcasted_iota(jnp.int32, (LANES, nc), 0)
    n_id = lax.broadcasted_iota(jnp.int32, (LANES, nc), 1)
    shift = ((u_id < 2 * per_blk) & (n_id == per_blk * (qi - 1) + u_id)).astype(BF16)
    cmp_future = lax.broadcasted_iota(jnp.int32, (rows, nc), 1) >= per_blk * (qi + 1)

    groups = range(KV_HEADS)
    qzs = [_stack_heads(qa_ref, g, nq) for g in groups]
    qz_b16 = [qz.astype(BF16) for qz in qzs]

    s = [_dot_nt(qz_b16[g], kc_ref[0, 0]) + _dot(tc_ref[0, g], shift) + _dot(tc_ref[1, g], shift)
         for g in groups]
    s = [jnp.where(cmp_future, NEG, s[g]) for g in groups]
    e = [jnp.exp(s[g] - jnp.max(s[g], axis=1, keepdims=True)) for g in groups]
    p = [jnp.where(s[g] > 0.5 * NEG, e[g], 0.0) / jnp.sum(e[g], axis=1, keepdims=True) for g in groups]
    o_cmp = [_dot(p[g].astype(BF16), vc_ref[0, 0]) for g in groups]

    works = []
    cand = lax.broadcasted_iota(jnp.int32, (LANES, nq), 0)
    own = (nq // NSA_SEL_LEN) * qi + lax.broadcasted_iota(jnp.int32, (LANES, nq), 1) // NSA_SEL_LEN
    for g in groups:
        p4 = p[g][0:nq] + p[g][nq:2 * nq] + p[g][2 * nq:3 * nq] + p[g][3 * nq:4 * nq]
        p_hi, p_lo = _split2(p4)
        imp = _dot_nt(ovw_ref[g], p_hi) + _dot_nt(ovw_ref[g], p_lo)
        j = cand - HEAD_DIM * (1 - g)
        forced = (j == 0) | (j == own) | (j == own - 1)
        imp = jnp.where(j <= own, imp + jnp.where(forced, NSA_FORCE, 0.0), NEG)
        works.append(jnp.where((cand // HEAD_DIM) == (1 - g), imp, BELOW_NEG))

    band = [_band_scores(qz_b16[g], kwin_pad_ref, twin_ref, g, qi, nq, NSA_WINDOW) for g in groups]
    pw = [_exp_sub(s_w, _rowmax(s_w)) for s_w, _ in band]
    for g in groups:
        o_win = _normalize(_dot(pw[g], vwin_aug_ref[g, band[g][1], :]))
        part_ref[g] = gate_col(g, 0) * o_cmp[g] + gate_col(g, 2) * o_win

    qz_b = [_stack_heads(qb_ref, g, nq).astype(BF16) for g in groups]
    band = [_band_scores(qz_b[g], kb_pad_ref, tswa_ref, g, qi, nq, SWA_WINDOW) for g in groups]
    m_b = [jnp.maximum(_rowmax(band[g][0]), sink_ref[g]) for g in groups]
    pb = [_exp_sub(band[g][0], m_b[g]) for g in groups]
    for g in groups:
        r = _dot(pb[g], vb_aug_ref[g, band[g][1], :])
        o_b = r / (pltpu.roll(r, HEAD_DIM, axis=1) + jnp.exp(sink_ref[g] - m_b[g]))
        o_ref[0, :, HQ + g * GROUP * HEAD_DIM:HQ + (g + 1) * GROUP * HEAD_DIM] = (
            _unstack_heads(o_b, g, nq).astype(BF16))

    picked = _pick_top(jnp.concatenate(works, axis=1), NSA_SEL_TOP)[0].T > 0.5
    far_blocks = (nq // NSA_SEL_LEN) * (qi - 1)
    for g in range(KV_HEADS):
        pk = picked[g * nq:(g + 1) * nq]
        j = lane - HEAD_DIM * (1 - g)
        sel_far = jnp.where(pk & (j < far_blocks), 0.0, MASK)
        sel_near = jnp.where(pk & (j >= far_blocks), 0.0, MASK)
        q_far_ref[g] = _with_aux(qzs[g], jnp.concatenate([sel_far] * GROUP, axis=0), g)
        q_near_ref[g] = _with_aux(qzs[g], jnp.concatenate([sel_near] * GROUP, axis=0), g)

    o_sel = _sweep(q_far_ref, q_near_ref, kaug_ref, vsel_aug_ref, tnear_ref, m_ref, acc_ref, qi, nq)
    for g in groups:
        o_a = part_ref[g] + gate_col(g, 1) * o_sel[g]
        o_ref[0, :, g * GROUP * HEAD_DIM:(g + 1) * GROUP * HEAD_DIM] = _unstack_heads(o_a, g, nq).astype(BF16)


def _nsa_swa(pb, pf, kvc, rel_table, sinks):
    b, s, _ = pb.shape
    nq = NSA_Q
    nc = s // NSA_CMP_STRIDE
    n_cmp = (s - NSA_CMP_LEN) // NSA_CMP_STRIDE + 1
    n_sel = s // NSA_SEL_LEN
    per_blk = nq // NSA_CMP_STRIDE
    assert n_sel <= HEAD_DIM and s % FAR_TILE == 0 and 2 * per_blk <= LANES
    h8 = HEADS_PER_MIXER
    f_a = _bias_by_dist(rel_table[:, :h8], NSA_WINDOW + nq)
    f_b = _bias_by_dist(rel_table[:, h8:2 * h8], NSA_WINDOW + nq)

    d_c = (jnp.arange(nq)[:, None] + nq - NSA_CMP_LEN + 1) - NSA_CMP_STRIDE * jnp.arange(2 * per_blk)[None, :]
    tc = jnp.where((d_c >= 0)[None], f_a[:, jnp.maximum(d_c, 0)], NEG)
    tc = jnp.pad(_rows_by_group(tc), ((0, 0), (0, 0), (0, LANES - 2 * per_blk)))
    tc_hi = tc.astype(BF16)
    tc = jnp.stack([tc_hi, (tc - tc_hi.astype(F32)).astype(BF16)])

    tnear = _near_tables(f_a, nq)
    twin = _band_table(f_a, nq, NSA_WINDOW)
    tswa = _band_table(f_b, nq, SWA_WINDOW)

    cmp_start = jnp.arange(nc) * NSA_CMP_STRIDE
    sel_start = jnp.arange(n_sel) * NSA_SEL_LEN
    overlap = jnp.maximum(jnp.minimum(cmp_start[:, None] + NSA_CMP_LEN, sel_start[None, :] + NSA_SEL_LEN)
                          - jnp.maximum(cmp_start[:, None], sel_start[None, :]), 0)
    overlap = jnp.where(jnp.arange(nc)[:, None] < n_cmp, overlap, 0).astype(F32) / NSA_CMP_LEN
    ovw = jnp.zeros((KV_HEADS, nc, LANES), F32)
    ovw = ovw.at[0, :, HEAD_DIM:HEAD_DIM + n_sel].set(overlap).at[1, :, :n_sel].set(overlap)
    ovw = ovw.transpose(0, 2, 1).astype(BF16)

    rows = GROUP * nq
    sink = (sinks.astype(F32) - rel_table[REL_BUCKETS - 1, h8:2 * h8]).reshape(KV_HEADS, GROUP, 1, 1)
    sink = jnp.broadcast_to(sink, (KV_HEADS, GROUP, nq, LANES)).reshape(KV_HEADS, rows, LANES)

    kv = lambda c: pl.BlockSpec((1, s, LANES), lambda i, j: (i, 0, c), pipeline_mode=pl.Buffered(1))
    aug = pltpu.VMEM((KV_HEADS, s, LANES), BF16)
    qsel = pltpu.VMEM((KV_HEADS, rows, LANES), BF16)
    return pl.pallas_call(
        _nsa_swa_kernel,
        grid=(b, s // nq),
        in_specs=[pl.BlockSpec((1, nq, HQ), lambda i, j: (i, j, 0)),
                  pl.BlockSpec((1, nq, HQ), lambda i, j: (i, j, 1)),
                  kv(8), kv(9), kv(10), kv(11), kv(12), kv(13),
                  pl.BlockSpec((1, nq, LANES), lambda i, j: (i, j, 2)),
                  pl.BlockSpec((1, 1, nc, LANES), lambda i, j: (i, 0, 0, 0)),
                  pl.BlockSpec((1, 1, nc, LANES), lambda i, j: (i, 1, 0, 0)),
                  _whole(tc), _whole(tnear), _whole(twin), _whole(tswa), _whole(ovw), _whole(sink)],
        out_specs=pl.BlockSpec((1, nq, 2 * HQ), lambda i, j: (i, j, 0)),
        out_shape=jax.ShapeDtypeStruct((b, s, 2 * HQ), BF16),
        scratch_shapes=[aug, aug,
                        pltpu.VMEM((s + NSA_WINDOW, LANES), BF16),
                        pltpu.VMEM((KV_HEADS, s + NSA_WINDOW, LANES), BF16),
                        pltpu.VMEM((s + SWA_WINDOW, LANES), BF16),
                        pltpu.VMEM((KV_HEADS, s + SWA_WINDOW, LANES), BF16),
                        qsel, qsel,
                        pltpu.VMEM((KV_HEADS, rows, LANES), F32),
                        pltpu.VMEM((KV_HEADS, rows, LANES), F32), pltpu.VMEM((KV_HEADS, rows, LANES), F32)],
        compiler_params=_cparams(("arbitrary", "arbitrary")),
        name="nsa_swa",
    )(pb, pb, pb, pb, pb, pb, pb, pb, pf, kvc, kvc, tc, tnear, twin, tswa, ovw, sink)


def _moba_kernel(q_ref, k_ref, v_ref, tnear_ref, o_ref, kaug_ref, vaug_ref, kmx_ref,
                 q_far_ref, q_near_ref, m_ref, acc_ref):
    qi = pl.program_id(1)
    nq = MOBA_BLOCK
    rows = GROUP * nq
    nb = k_ref.shape[1] // nq

    @pl.when(qi == 0)
    def _():
        _fill_aug(k_ref, kaug_ref, MOBA_BLOCK)
        _fill_aug(v_ref, vaug_ref, 0)
        kmx_ref[...] = jnp.zeros_like(kmx_ref)
        lane = lax.broadcasted_iota(jnp.int32, (1, LANES), 1)
        for blk in range(nb):
            mean = jnp.mean(k_ref[0, blk * nq:(blk + 1) * nq, :].astype(F32), axis=0, keepdims=True)
            for g in range(KV_HEADS):
                hi, lo = _split2(jnp.where((lane // HEAD_DIM) == g, mean, 0.0))
                r = HEAD_DIM * (1 - g) + blk
                kmx_ref[0, g, r:r + 1, :] = hi
                kmx_ref[1, g, r:r + 1, :] = lo

    lane = lax.broadcasted_iota(jnp.int32, (rows, LANES), 1)
    works, qzs = [], []
    nb_pad = -(-nb // 8) * 8
    j_blk = lax.broadcasted_iota(jnp.int32, (nb_pad, rows), 0)
    for g in range(KV_HEADS):
        qz = _stack_heads(q_ref, g, nq)
        qz_b16 = qz.astype(BF16)
        lo = HEAD_DIM * (1 - g)
        gs = (_dot_nt(kmx_ref[0, g, lo:lo + nb_pad, :], qz_b16)
              + _dot_nt(kmx_ref[1, g, lo:lo + nb_pad, :], qz_b16))
        works.append(jnp.where(j_blk < qi, gs, BELOW_NEG))
        qzs.append(qz)
    _, picked = _pick_top(jnp.concatenate(works, axis=1), MOBA_TOP)
    for g in range(KV_HEADS):
        lo = HEAD_DIM * (1 - g)
        parts = [jnp.zeros((lo, rows), F32)] * (lo > 0) + [picked[:, g * rows:(g + 1) * rows]]
        parts.append(jnp.zeros((LANES - lo - nb_pad, rows), F32))
        pk = jnp.concatenate(parts, axis=0).T > 0.5
        j = lane - HEAD_DIM * (1 - g)
        q_far_ref[g] = _with_aux(qzs[g], jnp.where(pk & (j < qi - 1), 0.0, MASK), g)
        q_near_ref[g] = _with_aux(qzs[g], jnp.where((pk & (j >= qi - 1)) | (j == qi), 0.0, MASK), g)

    o = _sweep(q_far_ref, q_near_ref, kaug_ref, vaug_ref, tnear_ref, m_ref, acc_ref, qi, nq)
    for g in range(KV_HEADS):
        o_ref[0, :, g * GROUP * HEAD_DIM:(g + 1) * GROUP * HEAD_DIM] = _unstack_heads(o[g], g, nq).astype(BF16)


def _moba(pb, rel_table, q_col, k_col, v_col):
    b, s, _ = pb.shape
    nq = MOBA_BLOCK
    nb = s // nq
    assert s % FAR_TILE == 0 and nb >= 2 and nb <= HEAD_DIM
    tnear = _near_tables(_bias_by_dist(rel_table[:, :HEADS_PER_MIXER], 2 * nq), nq)
    rows = GROUP * nq
    aug = pltpu.VMEM((KV_HEADS, s, LANES), BF16)
    qsel = pltpu.VMEM((KV_HEADS, rows, LANES), BF16)
    kv = lambda c: pl.BlockSpec((1, s, LANES), lambda i, j: (i, 0, c), pipeline_mode=pl.Buffered(1))
    return pl.pallas_call(
        _moba_kernel,
        grid=(b, nb),
        in_specs=[pl.BlockSpec((1, nq, HQ), lambda i, j: (i, j, q_col)), kv(k_col), kv(v_col), _whole(tnear)],
        out_specs=pl.BlockSpec((1, nq, HQ), lambda i, j: (i, j, 0)),
        out_shape=jax.ShapeDtypeStruct((b, s, HQ), BF16),
        scratch_shapes=[aug, aug,
                        pltpu.VMEM((2, KV_HEADS, LANES, LANES), BF16),
                        qsel, qsel,
                        pltpu.VMEM((KV_HEADS, rows, LANES), F32), pltpu.VMEM((KV_HEADS, rows, LANES), F32)],
        compiler_params=_cparams(("arbitrary", "arbitrary")),
        name="moba",
    )(pb, pb, pb, tnear)


def _stick_kernel(q_ref, k_ref, v_ref, tri_ref, o_ref, acc_ref, c_ref):
    qi = pl.program_id(1)
    t_len = SB_T
    pairs = HEADS_PER_MIXER // 2
    lane = lax.broadcasted_iota(jnp.int32, (t_len, LANES), 1)
    strict = lax.broadcasted_iota(jnp.int32, (t_len, t_len), 1) < lax.broadcasted_iota(jnp.int32, (t_len, t_len), 0)

    q_heads = []
    for pair in range(pairs):
        q = q_ref[0, :, pair * LANES:(pair + 1) * LANES].astype(F32) * SCALE
        q_heads += [jnp.where((lane // HEAD_DIM) == half, q, 0.0).astype(BF16) for half in range(2)]

    def tile(t, diag):
        rows = pl.ds(pl.multiple_of(t * t_len, t_len), t_len)
        heads = range(2 * pairs)
        zs = [_dot_nt(q_heads[h], k_ref[0, rows, (h // 2) * LANES:(h // 2 + 1) * LANES])
              for h in heads]
        logs = [-(jnp.maximum(z, 0.0) + jnp.log(1.0 + jnp.exp(-jnp.abs(z)))) for z in zs]
        if diag:
            logs = [jnp.where(strict, log_1m, 0.0) for log_1m in logs]
        local = [_dot(jnp.concatenate(_split2(log_1m), axis=1), tri_ref[...]) for log_1m in logs]
        for h in heads:
            total = jnp.broadcast_to(local[h][:, :1] + logs[h][:, :1], (t_len, LANES))
            c = jnp.zeros((t_len, LANES), F32) if diag else c_ref[h]
            after = local[h] + jnp.concatenate([c] * (t_len // LANES), axis=1)
            w = jnp.exp(logs[h] + zs[h] + after)
            v = v_ref[0, rows, (h // 2) * LANES:(h // 2 + 1) * LANES]
            if diag:
                acc_ref[h] = _dot(jnp.where(strict, w, 0.0).astype(BF16), v)
            else:
                acc_ref[h] += _dot(w.astype(BF16), v)
            c_ref[h] = c + total

    tile(qi, True)

    def cond(carry):
        t, c_max = carry
        return (t >= 0) & (c_max >= EXP_ZERO)

    def body(carry):
        t, _ = carry
        tile(t, False)
        return t - 1, jnp.max(c_ref[...])

    lax.while_loop(cond, body, (qi - 1, jnp.max(c_ref[...])))
    for pair in range(pairs):
        o_ref[0, :, pair * LANES:(pair + 1) * LANES] = jnp.where(
            (lane // HEAD_DIM) == 0, acc_ref[2 * pair], acc_ref[2 * pair + 1]).astype(BF16)


def _stick_breaking(pb, q_col, k_col, v_col):
    b, s, _ = pb.shape
    t_len = SB_T
    assert s % t_len == 0
    tri = (jnp.arange(t_len)[:, None] > jnp.arange(t_len)[None, :]).astype(BF16)
    tri = jnp.concatenate([tri, tri], axis=0)
    return pl.pallas_call(
        _stick_kernel,
        grid=(b, s // t_len),
        in_specs=[pl.BlockSpec((1, t_len, HQ), lambda i, j: (i, j, q_col)),
                  pl.BlockSpec((1, s, HQ), lambda i, j: (i, 0, k_col)),
                  pl.BlockSpec((1, s, HQ), lambda i, j: (i, 0, v_col)),
                  _whole(tri)],
        out_specs=pl.BlockSpec((1, t_len, HQ), lambda i, j: (i, j, 0)),
        out_shape=jax.ShapeDtypeStruct((b, s, HQ), BF16),
        scratch_shapes=[pltpu.VMEM((HEADS_PER_MIXER, t_len, LANES), F32),
                        pltpu.VMEM((HEADS_PER_MIXER, t_len, LANES), F32)],
        compiler_params=_cparams(("arbitrary", "arbitrary")),
        name="stick_breaking",
    )(pb, pb, pb, tri)


def _split_cols(w, sizes):
    offs = [0]
    for z in sizes:
        offs.append(offs[-1] + z)
    return [w[:, offs[i]:offs[i + 1]] for i in range(len(sizes))]


def kernel(x, c, rel_table, mod_w, mod_b, norm_w, w_in_ab, w_out_ab, nsa_cmp_wk, nsa_cmp_wv,
           nsa_cmp_pe, swa_sinks, w_in_cd, w_out_cd, ffn_w_in, ffn_w_out):
    depth = mod_w.shape[0]
    d = x.shape[-1]
    h8 = HEADS_PER_MIXER
    mod = _modulation(c, mod_w.reshape(depth * 2, d, 3 * d), mod_b.reshape(depth * 2, 3 * d))
    mod = mod.reshape(depth, 2, c.shape[0], 3, d)

    for layer in range(depth):
        shift, scale, gate = mod[layer, 0, :, 0], mod[layer, 0, :, 1], mod[layer, 0, :, 2]
        i = layer // 2
        if layer % 2 == 0:
            qa, kca, vca, ksa, vsa, kwa, vwa, ga, qb, kb, vb = _split_cols(
                w_in_ab[i], [HQ, KVW, KVW, KVW, KVW, KVW, KVW, 3 * h8, HQ, KVW, KVW])
            ga = jnp.pad(ga, ((0, 0), (0, LANES - 3 * h8)))
            w = jnp.concatenate([qa, qb, ksa, vsa, kwa, vwa, kb, vb, kca, vca, ga], axis=1).astype(BF16)
            nb = 2 * HQ + 6 * KVW
            pb, pf = _in_proj(x, norm_w[layer, 0, 0], scale, shift, w, nb, 3 * LANES, LANES)
            kvc = _compress(pf, nsa_cmp_wk[i], nsa_cmp_wv[i], nsa_cmp_pe[i])
            o = _nsa_swa(pb, pf, kvc, rel_table, swa_sinks[i])
            mixed, w_mix = (o, 0, o, 1), w_out_ab[i]
        else:
            qc, kc, vc, qd, kd, vd = _split_cols(w_in_cd[i], [HQ, KVW, KVW, HQ, HQ, HQ])
            w = jnp.concatenate([qc, qd, kd, vd, kc, vc], axis=1).astype(BF16)
            (pb,) = _in_proj(x, norm_w[layer, 0, 0], scale, shift, w, 4 * HQ + 2 * KVW, 0, 0)
            o_c = _moba(pb, rel_table, 0, 4 * HQ // LANES, 4 * HQ // LANES + 1)
            o_d = _stick_breaking(pb, 1, 2, 3)
            mixed, w_mix = (o_c, 0, o_d, 0), w_out_cd[i]

        shift, scale, gate_ffn = mod[layer, 1, :, 0], mod[layer, 1, :, 1], mod[layer, 1, :, 2]
        x = _mix_ffn(*mixed, w_mix.astype(BF16), x, norm_w[layer, 0, 1], gate,
                     norm_w[layer, 1, 0], scale, shift, ffn_w_in[layer].astype(BF16),
                     ffn_w_out[layer].astype(BF16), norm_w[layer, 1, 1], gate_ffn)
    return x
```
